```python
import math
import jax, jax.numpy as jnp
from jax import lax
import numpy as np

D_MODEL = 1024
BATCH = 2
SEQ = 8192
DEPTH = 4

EPS = 1e-5
ATTN_HEADS = 8
ATTN_KV_HEADS = 2
HEAD_DIM = 64
WINDOW = 128
ATTN_BLOCK = 128
ATTN_WIDTH = ATTN_HEADS * HEAD_DIM
KV_WIDTH = ATTN_KV_HEADS * HEAD_DIM
POOL_WINDOWS = (2, 4, 8, 16)
POOL_GROUPS = 4
POOL_WIDTH = D_MODEL // 2
POOL_GROUP_DIM = POOL_WIDTH // POOL_GROUPS
AP_IN = ATTN_WIDTH + 2 * KV_WIDTH + POOL_WIDTH
AP_OUT = ATTN_WIDTH + POOL_WIDTH
SSD_INNER = 2 * D_MODEL
SSD_HEAD_DIM = 64
SSD_HEADS = SSD_INNER // SSD_HEAD_DIM
SSD_GROUPS = 4
SSD_STATE = 128
SSD_CONV = 4
SSD_CHUNK = 128
SSD_CONV_DIM = SSD_INNER + 2 * SSD_GROUPS * SSD_STATE
SSD_IN = 2 * SSD_INNER + 2 * SSD_GROUPS * SSD_STATE + SSD_HEADS
FFN_HIDDEN = -(-8 * D_MODEL // (3 * 256)) * 256
N_EVEN = (DEPTH + 1) // 2
N_ODD = DEPTH // 2

kernel_name = "hybrid_swa_sink_pool_ssd_trunk"


def rmsnorm(x, w):
    xf = x.astype(jnp.float32)
    y = xf * lax.rsqrt(jnp.mean(xf * xf, axis=-1, keepdims=True) + EPS) * w.astype(jnp.float32)
    return y.astype(x.dtype)


def sliding_window_attention(q, k, v, sinks):
    b, s = q.shape[0], q.shape[1]
    nb = s // ATTN_BLOCK
    g = ATTN_HEADS // ATTN_KV_HEADS
    qb = q.reshape(b, nb, ATTN_BLOCK, ATTN_KV_HEADS, g, HEAD_DIM)

    def band(t):
        tp = jnp.pad(t, ((0, 0), (ATTN_BLOCK, 0), (0, 0), (0, 0)))
        tp = tp.reshape(b, nb + 1, ATTN_BLOCK, ATTN_KV_HEADS, HEAD_DIM)
        return jnp.concatenate([tp[:, :-1], tp[:, 1:]], axis=2)

    kb, vb = band(k), band(v)
    scores = jnp.einsum('bnqhgd,bnkhd->bnhgqk', qb, kb).astype(jnp.float32) * (HEAD_DIM ** -0.5)
    n_idx = jnp.arange(nb)[:, None, None]
    qi = jnp.arange(ATTN_BLOCK)[None, :, None]
    kj = jnp.arange(2 * ATTN_BLOCK)[None, None, :]
    diff = qi + ATTN_BLOCK - kj
    k_pos = (n_idx - 1) * ATTN_BLOCK + kj
    mask = (diff >= 0) & (diff < WINDOW) & (k_pos >= 0)
    scores = jnp.where(mask[None, :, None, None], scores, -1e30)
    sink = sinks.astype(jnp.float32).reshape(ATTN_KV_HEADS, g)[None, None, :, :, None, None]
    sink = jnp.broadcast_to(sink, scores.shape[:-1] + (1,))
    probs = jax.nn.softmax(jnp.concatenate([scores, sink], axis=-1), axis=-1)[..., :-1]
    out = jnp.einsum('bnhgqk,bnkhd->bnqhgd', probs.astype(vb.dtype), vb)
    return out.reshape(b, s, ATTN_WIDTH)


def multiscale_pool(u, pool_w, pool_scale):
    b, s, _ = u.shape
    uf = u.astype(jnp.float32)
    count = jnp.arange(1, s + 1, dtype=jnp.float32)[:, None]
    outs = []
    for gi, w in enumerate(POOL_WINDOWS):
        ug = uf[..., gi * POOL_GROUP_DIM:(gi + 1) * POOL_GROUP_DIM]
        cs = jnp.cumsum(ug, axis=1)
        lower = jnp.pad(cs[:, :s - w], ((0, 0), (w, 0), (0, 0)))
        mean = (cs - lower) / jnp.minimum(count, float(w))
        outs.append(mean - ug)
    d = jnp.stack(outs, axis=2)
    y = jnp.einsum('bsgc,gcd->bsgd', d, pool_w.astype(jnp.float32)).reshape(b, s, POOL_WIDTH)
    return (y * pool_scale.astype(jnp.float32)).astype(u.dtype)


def attn_pool_mixer(u, w_in, sinks, pool_w, pool_scale, w_out):
    b, s, _ = u.shape
    proj = u @ w_in
    q = proj[..., :ATTN_WIDTH].reshape(b, s, ATTN_HEADS, HEAD_DIM)
    k = proj[..., ATTN_WIDTH:ATTN_WIDTH + KV_WIDTH].reshape(b, s, ATTN_KV_HEADS, HEAD_DIM)
    v = proj[..., ATTN_WIDTH + KV_WIDTH:ATTN_WIDTH + 2 * KV_WIDTH].reshape(b, s, ATTN_KV_HEADS, HEAD_DIM)
    p = proj[..., ATTN_WIDTH + 2 * KV_WIDTH:]
    a = sliding_window_attention(q, k, v, sinks)
    m = multiscale_pool(p, pool_w, pool_scale)
    return jnp.concatenate([a, m], axis=-1) @ w_out


def causal_depthwise_conv(x, w, bias):
    s = x.shape[1]
    xp = jnp.pad(x, ((0, 0), (SSD_CONV - 1, 0), (0, 0)))
    out = xp[:, 0:s] * w[0]
    for kk in range(1, SSD_CONV):
        out = out + xp[:, kk:kk + s] * w[kk]
    return out + bias


def ssd_chunked(x, dt, A, Bm, Cm):
    b, s, h, p = x.shape
    L = SSD_CHUNK
    nc = s // L
    G = SSD_GROUPS
    hg = h // G
    n = SSD_STATE
    xdt = (x.astype(jnp.float32) * dt[..., None]).reshape(b, nc, L, G, hg, p)
    Bc = Bm.astype(jnp.float32).reshape(b, nc, L, G, n)
    Cc = Cm.astype(jnp.float32).reshape(b, nc, L, G, n)
    dA = (dt * A).reshape(b, nc, L, G, hg)
    a_cs = jnp.moveaxis(jnp.cumsum(dA, axis=2), 2, -1)
    seg = a_cs[..., :, None] - a_cs[..., None, :]
    causal = jnp.tril(jnp.ones((L, L), dtype=bool))
    decay = jnp.exp(jnp.where(causal, seg, -jnp.inf))
    cb = jnp.einsum('bclgn,bcsgn->bcgls', Cc, Bc)
    y_diag = jnp.einsum('bcgls,bcghls,bcsghp->bclghp', cb, decay, xdt)
    decay_states = jnp.exp(a_cs[..., -1:] - a_cs)
    states = jnp.einsum('bclgn,bcghl,bclghp->bcghpn', Bc, decay_states, xdt)
    chunk_decay = jnp.exp(a_cs[..., -1])

    def step(hstate, inp):
        st, dec = inp
        return hstate * dec[..., None, None] + st, hstate

    h0 = jnp.zeros((b, G, hg, p, n), jnp.float32)
    _, prev = lax.scan(step, h0, (jnp.moveaxis(states, 1, 0), jnp.moveaxis(chunk_decay, 1, 0)))
    prev = jnp.moveaxis(prev, 0, 1)
    y_off = jnp.einsum('bclgn,bcghpn,bcghl->bclghp', Cc, prev, jnp.exp(a_cs))
    return (y_diag + y_off).reshape(b, s, h, p)


def ssd_mixer(u, w_in, conv_w, conv_b, dt_bias, A_log, D, norm_w, w_out):
    b, s, _ = u.shape
    proj = u @ w_in
    z = proj[..., :SSD_INNER]
    xbc = proj[..., SSD_INNER:SSD_INNER + SSD_CONV_DIM]
    dt_raw = proj[..., SSD_INNER + SSD_CONV_DIM:]
    xbc = jax.nn.silu(causal_depthwise_conv(xbc, conv_w, conv_b))
    gn = SSD_GROUPS * SSD_STATE
    xs = xbc[..., :SSD_INNER].reshape(b, s, SSD_HEADS, SSD_HEAD_DIM)
    Bm = xbc[..., SSD_INNER:SSD_INNER + gn].reshape(b, s, SSD_GROUPS, SSD_STATE)
    Cm = xbc[..., SSD_INNER + gn:].reshape(b, s, SSD_GROUPS, SSD_STATE)
    dt = jax.nn.softplus(dt_raw.astype(jnp.float32) + dt_bias.astype(jnp.float32))
    A = -jnp.exp(A_log.astype(jnp.float32))
    y = ssd_chunked(xs, dt, A, Bm, Cm) + xs.astype(jnp.float32) * D.astype(jnp.float32)[:, None]
    y = y.reshape(b, s, SSD_INNER) * jax.nn.silu(z.astype(jnp.float32))
    yg = y.reshape(b, s, SSD_GROUPS, SSD_INNER // SSD_GROUPS)
    yg = yg * lax.rsqrt(jnp.mean(yg * yg, axis=-1, keepdims=True) + EPS)
    y = yg.reshape(b, s, SSD_INNER) * norm_w.astype(jnp.float32)
    return y.astype(u.dtype) @ w_out


def swiglu(h, w_gate, w_up, w_down):
    return (jax.nn.silu(h @ w_gate) * (h @ w_up)) @ w_down


def setup_inputs(seed: int = 0) -> dict:
    key = jax.random.key(seed)
    ks = jax.random.split(key, 24)
    f32 = jnp.float32

    def nrm(k, shape, scale):
        return jax.random.normal(k, shape, f32) * scale

    x = jax.random.normal(ks[0], (BATCH, SEQ, D_MODEL), f32)
    mix_norm_w = 1.0 + nrm(ks[1], (DEPTH, D_MODEL), 0.1)
    ap_w_in = nrm(ks[2], (N_EVEN, D_MODEL, AP_IN), D_MODEL ** -0.5)
    ap_sinks = nrm(ks[3], (N_EVEN, ATTN_HEADS), 0.5)
    pool_w = nrm(ks[4], (N_EVEN, POOL_GROUPS, POOL_GROUP_DIM, POOL_GROUP_DIM), POOL_GROUP_DIM ** -0.5)
    pool_scale = 1.0 + nrm(ks[5], (N_EVEN, POOL_WIDTH), 0.1)
    ap_w_out = nrm(ks[6], (N_EVEN, AP_OUT, D_MODEL), AP_OUT ** -0.5)
    ssd_w_in = nrm(ks[7], (N_ODD, D_MODEL, SSD_IN), D_MODEL ** -0.5)
    ssd_conv_w = nrm(ks[8], (N_ODD, SSD_CONV, SSD_CONV_DIM), SSD_CONV ** -0.5)
    ssd_conv_b = nrm(ks[9], (N_ODD, SSD_CONV_DIM), 0.02)
    dt0 = jnp.exp(jax.random.uniform(ks[10], (N_ODD, SSD_HEADS), f32, math.log(1e-3), math.log(1e-1)))
    ssd_dt_bias = dt0 + jnp.log(-jnp.expm1(-dt0))
    ssd_A_log = jnp.log(jax.random.uniform(ks[11], (N_ODD, SSD_HEADS), f32, 1.0, 16.0))
    ssd_D = 1.0 + nrm(ks[12], (N_ODD, SSD_HEADS), 0.1)
    ssd_norm_w = 1.0 + nrm(ks[13], (N_ODD, SSD_INNER), 0.1)
    ssd_w_out = nrm(ks[14], (N_ODD, SSD_INNER, D_MODEL), SSD_INNER ** -0.5)
    ffn_norm_w = 1.0 + nrm(ks[15], (DEPTH, D_MODEL), 0.1)
    w_gate = nrm(ks[16], (DEPTH, D_MODEL, FFN_HIDDEN), D_MODEL ** -0.5)
    w_up = nrm(ks[17], (DEPTH, D_MODEL, FFN_HIDDEN), D_MODEL ** -0.5)
    w_down = nrm(ks[18], (DEPTH, FFN_HIDDEN, D_MODEL), FFN_HIDDEN ** -0.5)
    final_norm_w = 1.0 + nrm(ks[19], (D_MODEL,), 0.1)
    return {"x": x, "mix_norm_w": mix_norm_w, "ap_w_in": ap_w_in, "ap_sinks": ap_sinks,
            "pool_w": pool_w, "pool_scale": pool_scale, "ap_w_out": ap_w_out,
            "ssd_w_in": ssd_w_in, "ssd_conv_w": ssd_conv_w, "ssd_conv_b": ssd_conv_b,
            "ssd_dt_bias": ssd_dt_bias, "ssd_A_log": ssd_A_log, "ssd_D": ssd_D,
            "ssd_norm_w": ssd_norm_w, "ssd_w_out": ssd_w_out, "ffn_norm_w": ffn_norm_w,
            "w_gate": w_gate, "w_up": w_up, "w_down": w_down, "final_norm_w": final_norm_w}


def reference(x, mix_norm_w, ap_w_in, ap_sinks, pool_w, pool_scale, ap_w_out,
              ssd_w_in, ssd_conv_w, ssd_conv_b, ssd_dt_bias, ssd_A_log, ssd_D,
              ssd_norm_w, ssd_w_out, ffn_norm_w, w_gate, w_up, w_down, final_norm_w):
    for layer in range(DEPTH):
        i = layer // 2
        h = rmsnorm(x, mix_norm_w[layer])
        if layer % 2 == 0:
            x = x + attn_pool_mixer(h, ap_w_in[i], ap_sinks[i], pool_w[i], pool_scale[i], ap_w_out[i])
        else:
            x = x + ssd_mixer(h, ssd_w_in[i], ssd_conv_w[i], ssd_conv_b[i], ssd_dt_bias[i],
                              ssd_A_log[i], ssd_D[i], ssd_norm_w[i], ssd_w_out[i])
        h = rmsnorm(x, ffn_norm_w[layer])
        x = x + swiglu(h, w_gate[layer], w_up[layer], w_down[layer])
    return rmsnorm(x, final_norm_w)
```

```python
import functools

import jax
import jax.numpy as jnp
from jax import lax
from jax.experimental import pallas as pl
from jax.experimental.pallas import tpu as pltpu

F32 = jnp.float32
BF16 = jnp.bfloat16

EPS = 1e-5
LANES = 128
ATTN_HEADS = 8
ATTN_KV_HEADS = 2
HEAD_DIM = 64
ATTN_BLOCK = 128
ATTN_GROUP = ATTN_HEADS // ATTN_KV_HEADS
ATTN_WIDTH = ATTN_HEADS * HEAD_DIM
KV_WIDTH = ATTN_KV_HEADS * HEAD_DIM
POOL_WINDOWS = (2, 4, 8, 16)
POOL_GROUP_DIM = 128
POOL_WIDTH = len(POOL_WINDOWS) * POOL_GROUP_DIM
POOL_CARRY = 16
SSD_HEAD_DIM = 64
SSD_GROUPS = 4
SSD_STATE = 128
SSD_CONV = 4
SSD_CHUNK = 128
CONV_CARRY = 8
NEG = -1e30

VMEM_LIMIT = 56 * 1024 * 1024

EVEN_TILE = 512
ODD_TILE = 256
FFN_TILE = 512


def _dot(a, b):
    return jnp.dot(a, b, preferred_element_type=F32)


def _dot_nt(a, b):
    return lax.dot_general(a, b, (((1,), (1,)), ((), ())), preferred_element_type=F32)


def _rmsnorm(x, w):
    ms = jnp.mean(x * x, axis=-1, keepdims=True)
    return x * lax.rsqrt(ms + EPS) * w


def _silu(x):
    return x * (1.0 / (1.0 + jnp.exp(-x)))


def _split3(v):
    hi = v.astype(BF16)
    r = v - hi.astype(F32)
    mid = r.astype(BF16)
    lo = (r - mid.astype(F32)).astype(BF16)
    return hi, mid, lo


def _const_spec(shape):
    nd = len(shape)
    return pl.BlockSpec(shape, lambda *_: (0,) * nd, pipeline_mode=pl.Buffered(1))


def _ffn_kernel(x_ref, nw_ref, wg_ref, wu_ref, wd_ref, fw_ref, o_ref, *, final):
    x = x_ref[...]
    h = _rmsnorm(x, nw_ref[...]).astype(BF16)
    g = _dot(h, wg_ref[...])
    u = _dot(h, wu_ref[...])
    a = (_silu(g) * u).astype(BF16)
    y = x + _dot(a, wd_ref[...])
    if final:
        y = _rmsnorm(y, fw_ref[...])
    o_ref[...] = y


def _ffn(x2d, nw, wg, wu, wd, fw, *, final, tile):
    n, d = x2d.shape
    hid = wg.shape[1]
    return pl.pallas_call(
        functools.partial(_ffn_kernel, final=final),
        grid=(n // tile,),
        in_specs=[
            pl.BlockSpec((tile, d), lambda i: (i, 0)),
            _const_spec((1, d)),
            _const_spec((d, hid)),
            _const_spec((d, hid)),
            _const_spec((hid, d)),
            _const_spec((1, d)),
        ],
        out_specs=pl.BlockSpec((tile, d), lambda i: (i, 0)),
        out_shape=jax.ShapeDtypeStruct((n, d), F32),
        compiler_params=pltpu.CompilerParams(
            dimension_semantics=("arbitrary",), vmem_limit_bytes=VMEM_LIMIT),
        name="ffn",
    )(x2d, nw, wg, wu, wd, fw)


def _even_kernel(x_ref, nw_ref, wq_ref, wk_ref, wv_ref, wp_ref, sink_ref, pw_ref, ps_ref,
                 woa_ref, wom_ref, o_ref, kprev_ref, vprev_ref, pcarry_ref, attn_ref, *, tile):
    t = pl.program_id(1)
    nblk = tile // ATTN_BLOCK
    blk = ATTN_BLOCK

    @pl.when(t == 0)
    def _():
        kprev_ref[...] = jnp.zeros_like(kprev_ref)
        vprev_ref[...] = jnp.zeros_like(vprev_ref)
        pcarry_ref[...] = jnp.zeros_like(pcarry_ref)

    x = x_ref[0]
    h = _rmsnorm(x, nw_ref[...]).astype(BF16)
    q = _dot(h, wq_ref[...]) * (HEAD_DIM ** -0.5)
    q_is_even = lax.broadcasted_iota(jnp.int32, q.shape, 1) % LANES < HEAD_DIM
    q_par = (jnp.where(q_is_even, q, 0.0).astype(BF16), jnp.where(q_is_even, 0.0, q).astype(BF16))
    k = _dot(h, wk_ref[...])
    v = _dot(h, wv_ref[...])
    p = _dot(h, wp_ref[...])

    lane = lax.broadcasted_iota(jnp.int32, (tile, KV_WIDTH), 1)
    lo_half = lane < HEAD_DIM

    def both_halves(a):
        sw = pltpu.roll(a, HEAD_DIM, 1)
        return (jnp.where(lo_half, a, sw).astype(BF16), jnp.where(lo_half, sw, a).astype(BF16))

    k2 = both_halves(k)
    v2 = both_halves(v)
    kfull = [jnp.concatenate([kprev_ref[hh], k2[hh]], axis=0) for hh in range(ATTN_KV_HEADS)]
    vfull = [jnp.concatenate([vprev_ref[hh], v2[hh]], axis=0) for hh in range(ATTN_KV_HEADS)]
    for hh in range(ATTN_KV_HEADS):
        kprev_ref[hh] = k2[hh][tile - blk:]
        vprev_ref[hh] = v2[hh][tile - blk:]

    rows = ATTN_GROUP * blk
    qi = lax.broadcasted_iota(jnp.int32, (rows, 2 * blk), 0) % blk
    kj = lax.broadcasted_iota(jnp.int32, (rows, 2 * blk), 1)
    band = (kj > qi) & (kj <= qi + blk)
    first_key = jnp.where(t == 0, blk, 0)
    qlane = lax.broadcasted_iota(jnp.int32, (blk, LANES), 1)
    q_lo = qlane < HEAD_DIM

    for j in range(nblk):
        valid = band & (kj >= first_key) if j == 0 else band
        for hh in range(ATTN_KV_HEADS):
            parts = []
            for g in range(ATTN_GROUP):
                c0 = (hh * ATTN_GROUP + g) // 2 * LANES
                parts.append(q_par[g % 2][j * blk:(j + 1) * blk, c0:c0 + LANES])
            qst = jnp.concatenate(parts, axis=0)
            kb = kfull[hh][j * blk:(j + 2) * blk]
            vb = vfull[hh][j * blk:(j + 2) * blk]
            s = _dot_nt(qst, kb)
            s = jnp.where(valid, s, NEG)
            sink = sink_ref[hh][:, 0:1]
            m = jnp.maximum(jnp.max(s, axis=-1, keepdims=True), sink)
            e = jnp.exp(s - m)
            den = jnp.sum(e, axis=-1, keepdims=True) + jnp.exp(sink - m)
            r = _dot(e.astype(BF16), vb) * (1.0 / den)
            for a in range(ATTN_GROUP // 2):
                slab = jnp.where(q_lo, r[(2 * a) * blk:(2 * a + 1) * blk],
                                 r[(2 * a + 1) * blk:(2 * a + 2) * blk])
                c0 = (hh * ATTN_GROUP // 2 + a) * LANES
                attn_ref[j * blk:(j + 1) * blk, c0:c0 + LANES] = slab.astype(BF16)

    pe = jnp.concatenate([pcarry_ref[...], p], axis=0)
    pcarry_ref[...] = p[tile - POOL_CARRY:]
    cnt = (t * tile + 1 + lax.broadcasted_iota(jnp.int32, (tile, POOL_GROUP_DIM), 0)).astype(F32)
    pooled = []
    for gi, w in enumerate(POOL_WINDOWS):
        u = pe[:, gi * POOL_GROUP_DIM:(gi + 1) * POOL_GROUP_DIM]
        sm = u
        sh = 1
        while sh < w:
            sm = sm + pltpu.roll(sm, sh, 0)
            sh *= 2
        mean = sm[POOL_CARRY:] / jnp.minimum(cnt, float(w))
        d = (mean - u[POOL_CARRY:]).astype(BF16)
        yg = _dot(d, pw_ref[gi]) * ps_ref[:, gi * POOL_GROUP_DIM:(gi + 1) * POOL_GROUP_DIM]
        pooled.append(yg.astype(BF16))
    mixed = jnp.concatenate(pooled, axis=1)
    o_ref[0] = x + _dot(attn_ref[...], woa_ref[...]) + _dot(mixed, wom_ref[...])


def _even_mixer(x, nw, wq, wk, wv, wp, sink_rep, pw, ps, woa, wom, *, tile):
    b, s, d = x.shape
    consts = (nw, wq, wk, wv, wp, sink_rep, pw, ps, woa, wom)
    return pl.pallas_call(
        functools.partial(_even_kernel, tile=tile),
        grid=(b, s // tile),
        in_specs=[pl.BlockSpec((1, tile, d), lambda i, j: (i, j, 0))]
        + [_const_spec(c.shape) for c in consts],
        out_specs=pl.BlockSpec((1, tile, d), lambda i, j: (i, j, 0)),
        out_shape=jax.ShapeDtypeStruct((b, s, d), F32),
        scratch_shapes=[
            pltpu.VMEM((ATTN_KV_HEADS, ATTN_BLOCK, KV_WIDTH), BF16),
            pltpu.VMEM((ATTN_KV_HEADS, ATTN_BLOCK, KV_WIDTH), BF16),
            pltpu.VMEM((POOL_CARRY, POOL_WIDTH), F32),
            pltpu.VMEM((tile, ATTN_WIDTH), BF16),
        ],
        compiler_params=pltpu.CompilerParams(
            dimension_semantics=("arbitrary", "arbitrary"), vmem_limit_bytes=VMEM_LIMIT),
        name="even_mixer",
    )(x, *consts)


def _softplus(x):
    return jnp.maximum(x, 0.0) + jnp.log(1.0 + jnp.exp(-jnp.abs(x)))


def _odd_kernel(x_ref, nw_ref, wz_ref, wxbc_ref, wdt_ref, wdtT_ref, cw_ref, cb_ref, dtb_ref, dtbT_ref,
                alog_ref, alogT_ref, dexp_ref, gnw_ref, expand_ref, wout_ref, o_ref,
                state_ref, ccarry_ref, y_ref, *, tile, inner, heads):
    t = pl.program_id(1)
    L = SSD_CHUNK
    nchunk = tile // L
    gn = SSD_GROUPS * SSD_STATE
    heads_per_group = heads // SSD_GROUPS
    pairs_per_group = heads_per_group // 2

    @pl.when(t == 0)
    def _():
        state_ref[...] = jnp.zeros_like(state_ref)
        ccarry_ref[...] = jnp.zeros_like(ccarry_ref)

    x = x_ref[0]
    hn = _rmsnorm(x, nw_ref[...]).astype(BF16)
    z = _dot(hn, wz_ref[...])
    xbc = _dot(hn, wxbc_ref[...])
    dtr = _dot(hn, wdt_ref[...])
    dtrT = _dot_nt(wdtT_ref[...], hn)

    xe = jnp.concatenate([ccarry_ref[...], xbc], axis=0)
    ccarry_ref[...] = xbc[tile - CONV_CARRY:]
    cw = cw_ref[...]
    acc = xe * cw[SSD_CONV - 1:SSD_CONV]
    for kk in range(1, SSD_CONV):
        acc = acc + pltpu.roll(xe, kk, 0) * cw[SSD_CONV - 1 - kk:SSD_CONV - kk]
    act = _silu(acc[CONV_CARRY:] + cb_ref[...])

    a_neg = -jnp.exp(alog_ref[...])
    a_negT = -jnp.exp(alogT_ref[...])

    li = lax.broadcasted_iota(jnp.int32, (L, L), 0)
    si = lax.broadcasted_iota(jnp.int32, (L, L), 1)
    causal = si <= li
    tril = jnp.where(causal, 1.0, 0.0).astype(BF16)
    triu = jnp.where(li <= si, 1.0, 0.0).astype(BF16)
    plane = lax.broadcasted_iota(jnp.int32, (L, LANES), 1)
    p_lo = plane < SSD_HEAD_DIM

    for c in range(nchunk):
        r0 = c * L
        dt = _softplus(dtr[r0:r0 + L] + dtb_ref[...])
        dtT = _softplus(dtrT[:, r0:r0 + L] + dtbT_ref[...])
        a_cs = sum(_dot(tril, part) for part in _split3(dt * a_neg))
        a_csT = sum(_dot(part, triu) for part in _split3(dtT * a_negT))
        a_last = a_cs[L - 1:L, :]
        a_lastT = a_csT[:, L - 1:L]
        wT = dtT * jnp.exp(a_lastT - a_csT)
        a_last16 = jnp.broadcast_to(a_last, (16, a_last.shape[1]))
        cd_row = jnp.exp(sum(_dot(part, expand_ref[...]) for part in _split3(a_last16))[0:1])

        for g in range(SSD_GROUPS):
            bg = act[r0:r0 + L, inner + g * SSD_STATE:inner + (g + 1) * SSD_STATE]
            cg = act[r0:r0 + L, inner + gn + g * SSD_STATE:inner + gn + (g + 1) * SSD_STATE]
            cb = _dot_nt(cg.astype(BF16), bg.astype(BF16))
            bgT = bg.T
            for pr in range(pairs_per_group):
                h0 = g * heads_per_group + 2 * pr
                c0 = h0 * SSD_HEAD_DIM
                xp = act[r0:r0 + L, c0:c0 + LANES]
                sp = state_ref[:, c0:c0 + LANES]
                w_a = jnp.concatenate([jnp.where(p_lo, xp, 0.0), jnp.where(p_lo, 0.0, xp)], axis=0).astype(BF16)
                w_b = jnp.concatenate([jnp.where(p_lo, sp, 0.0), jnp.where(p_lo, 0.0, sp)], axis=0).astype(BF16)
                m_parts, b_parts, c_parts = [], [], []
                for hd in (h0, h0 + 1):
                    acol = jnp.broadcast_to(a_cs[:, hd:hd + 1], (L, L))
                    arow = a_csT[hd:hd + 1, :]
                    dec = jnp.exp(jnp.where(causal, acol - arow, NEG))
                    m_parts.append((cb * dec * dtT[hd:hd + 1, :]).astype(BF16))
                    c_parts.append((cg * jnp.exp(acol)).astype(BF16))
                    b_parts.append((bgT * wT[hd:hd + 1, :]).astype(BF16))
                lhs_a = jnp.concatenate([jnp.concatenate(m_parts, axis=1),
                                         jnp.concatenate(b_parts, axis=1)], axis=0)
                ra = _dot(lhs_a, w_a)
                rb = _dot(jnp.concatenate(c_parts, axis=1), w_b)
                y_ref[r0:r0 + L, c0:c0 + LANES] = ra[:L] + rb
                state_ref[:, c0:c0 + LANES] = sp * cd_row[:, c0:c0 + LANES] + ra[L:]

    y = y_ref[...] + act[:, :inner] * dexp_ref[...]
    y = y * _silu(z)
    gw = inner // SSD_GROUPS
    outs = []
    for g in range(SSD_GROUPS):
        yg = y[:, g * gw:(g + 1) * gw]
        ms = jnp.mean(yg * yg, axis=-1, keepdims=True)
        outs.append((yg * lax.rsqrt(ms + EPS) * gnw_ref[:, g * gw:(g + 1) * gw]).astype(BF16))
    yb = jnp.concatenate(outs, axis=1)
    o_ref[0] = x + _dot(yb, wout_ref[...])


def _odd_mixer(x, nw, wz, wxbc, wdt, wdtT, cw, cb, dtb, dtbT, alog, alogT, dexp, gnw, expand, wout, *, tile):
    b, s, d = x.shape
    inner = wz.shape[1]
    heads = wdtT.shape[0]
    conv_dim = wxbc.shape[1]
    consts = (nw, wz, wxbc, wdt, wdtT, cw, cb, dtb, dtbT, alog, alogT, dexp, gnw, expand, wout)
    return pl.pallas_call(
        functools.partial(_odd_kernel, tile=tile, inner=inner, heads=heads),
        grid=(b, s // tile),
        in_specs=[pl.BlockSpec((1, tile, d), lambda i, j: (i, j, 0))]
        + [_const_spec(c.shape) for c in consts],
        out_specs=pl.BlockSpec((1, tile, d), lambda i, j: (i, j, 0)),
        out_shape=jax.ShapeDtypeStruct((b, s, d), F32),
        scratch_shapes=[
            pltpu.VMEM((SSD_STATE, inner), F32),
            pltpu.VMEM((CONV_CARRY, conv_dim), F32),
            pltpu.VMEM((tile, inner), F32),
        ],
        compiler_params=pltpu.CompilerParams(
            dimension_semantics=("arbitrary", "arbitrary"), vmem_limit_bytes=VMEM_LIMIT),
        name="odd_mixer",
    )(x, *consts)


def _row(v):
    return v.reshape(1, -1).astype(F32)


def _col(v):
    return v.reshape(-1, 1).astype(F32)


def kernel(x, mix_norm_w, ap_w_in, ap_sinks, pool_w, pool_scale, ap_w_out, ssd_w_in, ssd_conv_w, ssd_conv_b, ssd_dt_bias, ssd_A_log, ssd_D, ssd_norm_w, ssd_w_out, ffn_norm_w, w_gate, w_up, w_down, final_norm_w):
    b, s, d = x.shape
    depth = mix_norm_w.shape[0]
    heads = ssd_dt_bias.shape[1]
    inner = heads * SSD_HEAD_DIM
    conv_dim = inner + 2 * SSD_GROUPS * SSD_STATE
    even_tile = min(EVEN_TILE, s)
    odd_tile = min(ODD_TILE, s)
    ffn_tile = min(FFN_TILE, b * s)
    def lane_pad(a):
        return jnp.pad(a, ((0, 0), (0, LANES - heads)))

    expand = jnp.pad(jnp.repeat(jnp.eye(heads, dtype=BF16), SSD_HEAD_DIM, axis=1),
                     ((0, LANES - heads), (0, 0)))

    for layer in range(depth):
        i = layer // 2
        nw = _row(mix_norm_w[layer])
        if layer % 2 == 0:
            w_in = ap_w_in[i].astype(BF16)
            wq = w_in[:, :ATTN_WIDTH]
            wk = w_in[:, ATTN_WIDTH:ATTN_WIDTH + KV_WIDTH]
            wv = w_in[:, ATTN_WIDTH + KV_WIDTH:ATTN_WIDTH + 2 * KV_WIDTH]
            wp = w_in[:, ATTN_WIDTH + 2 * KV_WIDTH:]
            sink_rep = jnp.broadcast_to(
                ap_sinks[i].astype(F32).reshape(ATTN_KV_HEADS, ATTN_GROUP, 1, 1),
                (ATTN_KV_HEADS, ATTN_GROUP, ATTN_BLOCK, LANES)).reshape(ATTN_KV_HEADS, ATTN_GROUP * ATTN_BLOCK, LANES)
            w_out = ap_w_out[i].astype(BF16)
            x = _even_mixer(x, nw, wq, wk, wv, wp, sink_rep, pool_w[i].astype(BF16), _row(pool_scale[i]),
                            w_out[:ATTN_WIDTH], w_out[ATTN_WIDTH:], tile=even_tile)
        else:
            w_in = ssd_w_in[i].astype(BF16)
            wz = w_in[:, :inner]
            wxbc = w_in[:, inner:inner + conv_dim]
            wdt = w_in[:, inner + conv_dim:]
            x = _odd_mixer(x, nw, wz, wxbc, lane_pad(wdt), wdt.T, ssd_conv_w[i].astype(F32), _row(ssd_conv_b[i]),
                           lane_pad(_row(ssd_dt_bias[i])), _col(ssd_dt_bias[i]),
                           lane_pad(_row(ssd_A_log[i])), _col(ssd_A_log[i]),
                           _row(jnp.repeat(ssd_D[i], SSD_HEAD_DIM)), _row(ssd_norm_w[i]), expand,
                           ssd_w_out[i].astype(BF16), tile=odd_tile)
        final = layer == depth - 1
        x = _ffn(x.reshape(b * s, d), _row(ffn_norm_w[layer]), w_gate[layer].astype(BF16),
                 w_up[layer].astype(BF16), w_down[layer].astype(BF16), _row(final_norm_w),
                 final=final, tile=ffn_tile).reshape(b, s, d)
    return x
```

```python
import functools

import jax
import jax.numpy as jnp
from jax import lax
from jax.experimental import pallas as pl
from jax.experimental.pallas import tpu as pltpu

F32 = jnp.float32
BF16 = jnp.bfloat16

EPS = 1e-5
LANES = 128
ATTN_HEADS = 8
ATTN_KV_HEADS = 2
HEAD_DIM = 64
ATTN_BLOCK = 128
ATTN_GROUP = ATTN_HEADS // ATTN_KV_HEADS
ATTN_WIDTH = ATTN_HEADS * HEAD_DIM
KV_WIDTH = ATTN_KV_HEADS * HEAD_DIM
POOL_WINDOWS = (2, 4, 8, 16)
POOL_GROUP_DIM = 128
POOL_WIDTH = len(POOL_WINDOWS) * POOL_GROUP_DIM
POOL_CARRY = 16
SSD_HEAD_DIM = 64
SSD_GROUPS = 4
SSD_STATE = 128
SSD_CONV = 4
SSD_CHUNK = 128
CONV_CARRY = 8
NEG = -1e30

VMEM_LIMIT = 56 * 1024 * 1024
CAST_BLOCK_BYTES = 6 * 1024 * 1024

EVEN_TILE = 512
ODD_TILE = 256
FFN_TILE = 512


def _dot(a, b):
    return jnp.dot(a, b, preferred_element_type=F32)


def _dot_nt(a, b):
    return lax.dot_general(a, b, (((1,), (1,)), ((), ())), preferred_element_type=F32)


def _rmsnorm(x, w):
    ms = jnp.mean(x * x, axis=-1, keepdims=True)
    return x * lax.rsqrt(ms + EPS) * w


def _silu(x):
    h = 0.5 * x
    return h + h * jnp.tanh(h)


def _split3(v):
    hi = v.astype(BF16)
    r = v - hi.astype(F32)
    mid = r.astype(BF16)
    lo = (r - mid.astype(F32)).astype(BF16)
    return hi, mid, lo


def _const_spec(shape):
    nd = len(shape)
    return pl.BlockSpec(shape, lambda *_: (0,) * nd, pipeline_mode=pl.Buffered(1))


def _layer_spec(w, layer):
    return pl.BlockSpec((None,) + w.shape[1:], lambda *_: (layer, 0, 0), pipeline_mode=pl.Buffered(1))


def _cast_kernel(x_ref, o_ref):
    o_ref[...] = x_ref[...].astype(BF16)


def _to_bf16(w):
    nl, r, c = w.shape
    rb = r
    while rb * c * 4 > CAST_BLOCK_BYTES and rb % 32 == 0:
        rb //= 2
    return pl.pallas_call(
        _cast_kernel,
        grid=(nl, r // rb),
        in_specs=[pl.BlockSpec((1, rb, c), lambda i, j: (i, j, 0))],
        out_specs=pl.BlockSpec((1, rb, c), lambda i, j: (i, j, 0)),
        out_shape=jax.ShapeDtypeStruct(w.shape, BF16),
        compiler_params=pltpu.CompilerParams(
            dimension_semantics=("arbitrary", "arbitrary"), vmem_limit_bytes=VMEM_LIMIT),
        name="cast_bf16",
    )(w)


def _ffn_kernel(x_ref, nw_ref, wg_ref, wu_ref, wd_ref, fw_ref, o_ref, *, final):
    x = x_ref[...]
    h = _rmsnorm(x, nw_ref[...]).astype(BF16)
    g = _dot(h, wg_ref[...])
    u = _dot(h, wu_ref[...])
    a = (_silu(g) * u).astype(BF16)
    y = x + _dot(a, wd_ref[...])
    if final:
        y = _rmsnorm(y, fw_ref[...])
    o_ref[...] = y


def _ffn(x2d, nw, wg, wu, wd, fw, layer, *, final, tile):
    n, d = x2d.shape
    return pl.pallas_call(
        functools.partial(_ffn_kernel, final=final),
        grid=(n // tile,),
        in_specs=[
            pl.BlockSpec((tile, d), lambda i: (i, 0)),
            _const_spec((1, d)),
            _layer_spec(wg, layer),
            _layer_spec(wu, layer),
            _layer_spec(wd, layer),
            _const_spec((1, d)),
        ],
        out_specs=pl.BlockSpec((tile, d), lambda i: (i, 0)),
        out_shape=jax.ShapeDtypeStruct((n, d), F32),
        compiler_params=pltpu.CompilerParams(
            dimension_semantics=("arbitrary",), vmem_limit_bytes=VMEM_LIMIT),
        name="ffn",
    )(x2d, nw, wg, wu, wd, fw)


def _even_kernel(x_ref, nw_ref, win_ref, sink_ref, pw_ref, ps_ref, wout_ref, o_ref,
                 kprev_ref, vprev_ref, pcarry_ref, attn_ref, *, tile):
    t = pl.program_id(1)
    nblk = tile // ATTN_BLOCK
    blk = ATTN_BLOCK

    @pl.when(t == 0)
    def _():
        kprev_ref[...] = jnp.zeros_like(kprev_ref)
        vprev_ref[...] = jnp.zeros_like(vprev_ref)
        pcarry_ref[...] = jnp.zeros_like(pcarry_ref)

    x = x_ref[0]
    h = _rmsnorm(x, nw_ref[...]).astype(BF16)
    proj = _dot(h, win_ref[...])
    q = proj[:, :ATTN_WIDTH] * (HEAD_DIM ** -0.5)
    k = proj[:, ATTN_WIDTH:ATTN_WIDTH + KV_WIDTH]
    v = proj[:, ATTN_WIDTH + KV_WIDTH:ATTN_WIDTH + 2 * KV_WIDTH]
    p = proj[:, ATTN_WIDTH + 2 * KV_WIDTH:]
    q_is_even = lax.broadcasted_iota(jnp.int32, q.shape, 1) % LANES < HEAD_DIM
    q_par = (jnp.where(q_is_even, q, 0.0).astype(BF16), jnp.where(q_is_even, 0.0, q).astype(BF16))

    lane = lax.broadcasted_iota(jnp.int32, (tile, KV_WIDTH), 1)
    lo_half = lane < HEAD_DIM

    def both_halves(a):
        sw = pltpu.roll(a, HEAD_DIM, 1)
        return (jnp.where(lo_half, a, sw).astype(BF16), jnp.where(lo_half, sw, a).astype(BF16))

    k2 = both_halves(k)
    v2 = both_halves(v)
    kfull = [jnp.concatenate([kprev_ref[hh], k2[hh]], axis=0) for hh in range(ATTN_KV_HEADS)]
    vfull = [jnp.concatenate([vprev_ref[hh], v2[hh]], axis=0) for hh in range(ATTN_KV_HEADS)]
    for hh in range(ATTN_KV_HEADS):
        kprev_ref[hh] = k2[hh][tile - blk:]
        vprev_ref[hh] = v2[hh][tile - blk:]

    rows = ATTN_GROUP * blk
    qi = lax.broadcasted_iota(jnp.int32, (rows, 2 * blk), 0) % blk
    kj = lax.broadcasted_iota(jnp.int32, (rows, 2 * blk), 1)
    band = (kj > qi) & (kj <= qi + blk)
    first_key = jnp.where(t == 0, blk, 0)
    qlane = lax.broadcasted_iota(jnp.int32, (blk, LANES), 1)
    q_lo = qlane < HEAD_DIM
    ones = jnp.ones((2 * blk, LANES), BF16)

    for j in range(nblk):
        valid = band & (kj >= first_key) if j == 0 else band
        for hh in range(ATTN_KV_HEADS):
            parts = []
            for g in range(ATTN_GROUP):
                c0 = (hh * ATTN_GROUP + g) // 2 * LANES
                parts.append(q_par[g % 2][j * blk:(j + 1) * blk, c0:c0 + LANES])
            qst = jnp.concatenate(parts, axis=0)
            kb = kfull[hh][j * blk:(j + 2) * blk]
            vb = jnp.concatenate([vfull[hh][j * blk:(j + 2) * blk], ones], axis=1)
            s = _dot_nt(qst, kb)
            s = jnp.where(valid, s, NEG)
            sink = sink_ref[hh]
            m = jnp.maximum(jnp.max(s, axis=-1, keepdims=True), sink)
            e = jnp.exp(s - jnp.concatenate([m, m], axis=1))
            r = _dot(e.astype(BF16), vb)
            den = r[:, LANES:] + jnp.exp(sink - m)
            o = r[:, :LANES] * (1.0 / den)
            for a in range(ATTN_GROUP // 2):
                slab = jnp.where(q_lo, o[(2 * a) * blk:(2 * a + 1) * blk],
                                 o[(2 * a + 1) * blk:(2 * a + 2) * blk])
                c0 = (hh * ATTN_GROUP // 2 + a) * LANES
                attn_ref[j * blk:(j + 1) * blk, c0:c0 + LANES] = slab.astype(BF16)

    pe = jnp.concatenate([pcarry_ref[...], p], axis=0)
    pcarry_ref[...] = p[tile - POOL_CARRY:]
    cnt = (t * tile + 1 + lax.broadcasted_iota(jnp.int32, (tile, POOL_GROUP_DIM), 0)).astype(F32)
    pooled = []
    for gi, w in enumerate(POOL_WINDOWS):
        u = pe[:, gi * POOL_GROUP_DIM:(gi + 1) * POOL_GROUP_DIM]
        sm = u
        sh = 1
        while sh < w:
            sm = sm + pltpu.roll(sm, sh, 0)
            sh *= 2
        mean = sm[POOL_CARRY:] / jnp.minimum(cnt, float(w))
        d = (mean - u[POOL_CARRY:]).astype(BF16)
        yg = _dot(d, pw_ref[gi]) * ps_ref[:, gi * POOL_GROUP_DIM:(gi + 1) * POOL_GROUP_DIM]
        pooled.append(yg.astype(BF16))
    mixed = jnp.concatenate(pooled, axis=1)
    o_ref[0] = (x + _dot(attn_ref[...], wout_ref[:ATTN_WIDTH, :])
                + _dot(mixed, wout_ref[ATTN_WIDTH:, :]))


def _even_mixer(x, nw, w_in, sink_rep, pw, ps, w_out, layer, *, tile):
    b, s, d = x.shape
    return pl.pallas_call(
        functools.partial(_even_kernel, tile=tile),
        grid=(b, s // tile),
        in_specs=[
            pl.BlockSpec((1, tile, d), lambda i, j: (i, j, 0)),
            _const_spec(nw.shape),
            _layer_spec(w_in, layer),
            _const_spec(sink_rep.shape),
            _const_spec(pw.shape),
            _const_spec(ps.shape),
            _layer_spec(w_out, layer),
        ],
        out_specs=pl.BlockSpec((1, tile, d), lambda i, j: (i, j, 0)),
        out_shape=jax.ShapeDtypeStruct((b, s, d), F32),
        scratch_shapes=[
            pltpu.VMEM((ATTN_KV_HEADS, ATTN_BLOCK, KV_WIDTH), BF16),
            pltpu.VMEM((ATTN_KV_HEADS, ATTN_BLOCK, KV_WIDTH), BF16),
            pltpu.VMEM((POOL_CARRY, POOL_WIDTH), F32),
            pltpu.VMEM((tile, ATTN_WIDTH), BF16),
        ],
        compiler_params=pltpu.CompilerParams(
            dimension_semantics=("arbitrary", "arbitrary"), vmem_limit_bytes=VMEM_LIMIT),
        name="even_mixer",
    )(x, nw, w_in, sink_rep, pw, ps, w_out)


def _softplus(x):
    return jnp.maximum(x, 0.0) + jnp.log(1.0 + jnp.exp(-jnp.abs(x)))


def _odd_kernel(x_ref, nw_ref, win_ref, wdt_ref, wdtT_ref, cw_ref, cb_ref, dtb_ref, dtbT_ref,
                alog_ref, alogT_ref, dexp_ref, gnw_ref, expand_ref, wout_ref, o_ref,
                state_ref, ccarry_ref, y_ref, *, tile, inner, heads):
    t = pl.program_id(1)
    L = SSD_CHUNK
    nchunk = tile // L
    gn = SSD_GROUPS * SSD_STATE
    conv_dim = inner + 2 * gn
    heads_per_group = heads // SSD_GROUPS
    pairs_per_group = heads_per_group // 2
    group_width = heads_per_group * SSD_HEAD_DIM

    @pl.when(t == 0)
    def _():
        state_ref[...] = jnp.zeros_like(state_ref)
        ccarry_ref[...] = jnp.zeros_like(ccarry_ref)

    x = x_ref[0]
    hn = _rmsnorm(x, nw_ref[...]).astype(BF16)
    z = _dot(hn, win_ref[:, :inner])
    xbc = _dot(hn, win_ref[:, inner:inner + conv_dim])
    dtr = _dot(hn, wdt_ref[...])
    dtrT = _dot_nt(wdtT_ref[...], hn)

    xe = jnp.concatenate([ccarry_ref[...], xbc], axis=0)
    ccarry_ref[...] = xbc[tile - CONV_CARRY:]
    cw = cw_ref[...]
    acc = xe * cw[SSD_CONV - 1:SSD_CONV]
    for kk in range(1, SSD_CONV):
        acc = acc + pltpu.roll(xe, kk, 0) * cw[SSD_CONV - 1 - kk:SSD_CONV - kk]
    act = _silu(acc[CONV_CARRY:] + cb_ref[...])

    a_neg = -jnp.exp(alog_ref[...])
    a_negT = -jnp.exp(alogT_ref[...])

    li = lax.broadcasted_iota(jnp.int32, (L, L), 0)
    si = lax.broadcasted_iota(jnp.int32, (L, L), 1)
    causal = si <= li
    tril = jnp.where(causal, 1.0, 0.0).astype(BF16)
    triu = jnp.where(li <= si, 1.0, 0.0).astype(BF16)
    plane = lax.broadcasted_iota(jnp.int32, (L, LANES), 1)
    p_lo = plane < SSD_HEAD_DIM

    for c in range(nchunk):
        r0 = c * L
        dt = _softplus(dtr[r0:r0 + L] + dtb_ref[...])
        dtT = _softplus(dtrT[:, r0:r0 + L] + dtbT_ref[...])
        a_cs = sum(_dot(tril, part) for part in _split3(dt * a_neg))
        a_csT = sum(_dot(part, triu) for part in _split3(dtT * a_negT))
        a_last = a_cs[L - 1:L, :]
        a_lastT = a_csT[:, L - 1:L]
        wT = dtT * jnp.exp(a_lastT - a_csT)
        a_shiftT = a_csT - jnp.log(dtT)
        ea = jnp.exp(a_cs)
        a_last16 = jnp.broadcast_to(a_last, (16, a_last.shape[1]))
        cd_row = jnp.exp(sum(_dot(part, expand_ref[...]) for part in _split3(a_last16))[0:1])

        for g in range(SSD_GROUPS):
            bg = act[r0:r0 + L, inner + g * SSD_STATE:inner + (g + 1) * SSD_STATE]
            cg = act[r0:r0 + L, inner + gn + g * SSD_STATE:inner + gn + (g + 1) * SSD_STATE].astype(BF16)
            cb = _dot_nt(cg, bg.astype(BF16))
            bgT = bg.T
            g0 = g * group_width
            y_off = _dot(cg, state_ref[:, g0:g0 + group_width].astype(BF16))
            for pr in range(pairs_per_group):
                h0 = g * heads_per_group + 2 * pr
                c0 = h0 * SSD_HEAD_DIM
                xp = act[r0:r0 + L, c0:c0 + LANES]
                w_a = jnp.concatenate([jnp.where(p_lo, xp, 0.0), jnp.where(p_lo, 0.0, xp)], axis=0).astype(BF16)
                m_parts, b_parts, e_parts = [], [], []
                for hd in (h0, h0 + 1):
                    acol = jnp.broadcast_to(a_cs[:, hd:hd + 1], (L, L))
                    dec = jnp.exp(jnp.where(causal, acol - a_shiftT[hd:hd + 1, :], NEG))
                    m_parts.append((cb * dec).astype(BF16))
                    b_parts.append((bgT * wT[hd:hd + 1, :]).astype(BF16))
                    e_parts.append(jnp.broadcast_to(ea[:, hd:hd + 1], (L, LANES)))
                lhs = jnp.concatenate([jnp.concatenate(m_parts, axis=1),
                                       jnp.concatenate(b_parts, axis=1)], axis=0)
                ra = _dot(lhs, w_a)
                ea_pair = jnp.where(p_lo, e_parts[0], e_parts[1])
                y_ref[r0:r0 + L, c0:c0 + LANES] = ra[:L] + y_off[:, c0 - g0:c0 - g0 + LANES] * ea_pair
                state_ref[:, c0:c0 + LANES] = (state_ref[:, c0:c0 + LANES] * cd_row[:, c0:c0 + LANES] + ra[L:])

    y = y_ref[...] + act[:, :inner] * dexp_ref[...]
    y = y * _silu(z)
    gw = inner // SSD_GROUPS
    outs = []
    for g in range(SSD_GROUPS):
        yg = y[:, g * gw:(g + 1) * gw]
        ms = jnp.mean(yg * yg, axis=-1, keepdims=True)
        outs.append((yg * lax.rsqrt(ms + EPS) * gnw_ref[:, g * gw:(g + 1) * gw]).astype(BF16))
    yb = jnp.concatenate(outs, axis=1)
    o_ref[0] = x + _dot(yb, wout_ref[...])


def _odd_mixer(x, nw, w_in, wdt, wdtT, cw, cb, dtb, dtbT, alog, alogT, dexp, gnw, expand, w_out, layer, *, tile):
    b, s, d = x.shape
    heads = wdtT.shape[0]
    inner = heads * SSD_HEAD_DIM
    conv_dim = inner + 2 * SSD_GROUPS * SSD_STATE
    small = (wdt, wdtT, cw, cb, dtb, dtbT, alog, alogT, dexp, gnw, expand)
    return pl.pallas_call(
        functools.partial(_odd_kernel, tile=tile, inner=inner, heads=heads),
        grid=(b, s // tile),
        in_specs=[pl.BlockSpec((1, tile, d), lambda i, j: (i, j, 0)), _const_spec(nw.shape),
                  _layer_spec(w_in, layer)]
        + [_const_spec(c.shape) for c in small] + [_layer_spec(w_out, layer)],
        out_specs=pl.BlockSpec((1, tile, d), lambda i, j: (i, j, 0)),
        out_shape=jax.ShapeDtypeStruct((b, s, d), F32),
        scratch_shapes=[
            pltpu.VMEM((SSD_STATE, inner), F32),
            pltpu.VMEM((CONV_CARRY, conv_dim), F32),
            pltpu.VMEM((tile, inner), F32),
        ],
        compiler_params=pltpu.CompilerParams(
            dimension_semantics=("arbitrary", "arbitrary"), vmem_limit_bytes=VMEM_LIMIT),
        name="odd_mixer",
    )(x, nw, w_in, *small, w_out)


def _row(v):
    return v.reshape(1, -1).astype(F32)


def _col(v):
    return v.reshape(-1, 1).astype(F32)


def kernel(x, mix_norm_w, ap_w_in, ap_sinks, pool_w, pool_scale, ap_w_out, ssd_w_in, ssd_conv_w, ssd_conv_b, ssd_dt_bias, ssd_A_log, ssd_D, ssd_norm_w, ssd_w_out, ffn_norm_w, w_gate, w_up, w_down, final_norm_w):
    b, s, d = x.shape
    depth = mix_norm_w.shape[0]
    heads = ssd_dt_bias.shape[1]
    inner = heads * SSD_HEAD_DIM
    conv_dim = inner + 2 * SSD_GROUPS * SSD_STATE
    even_tile = min(EVEN_TILE, s)
    odd_tile = min(ODD_TILE, s)
    ffn_tile = min(FFN_TILE, b * s)

    ap_w_in_b, ap_w_out_b = _to_bf16(ap_w_in), _to_bf16(ap_w_out)
    ssd_w_in_b, ssd_w_out_b = _to_bf16(ssd_w_in), _to_bf16(ssd_w_out)
    w_gate_b, w_up_b, w_down_b = _to_bf16(w_gate), _to_bf16(w_up), _to_bf16(w_down)

    def lane_pad(a):
        return jnp.pad(a, ((0, 0), (0, LANES - heads)))

    expand = jnp.pad(jnp.repeat(jnp.eye(heads, dtype=BF16), SSD_HEAD_DIM, axis=1),
                     ((0, LANES - heads), (0, 0)))

    for layer in range(depth):
        i = layer // 2
        nw = _row(mix_norm_w[layer])
        if layer % 2 == 0:
            sink_rep = jnp.broadcast_to(
                ap_sinks[i].astype(F32).reshape(ATTN_KV_HEADS, ATTN_GROUP, 1, 1),
                (ATTN_KV_HEADS, ATTN_GROUP, ATTN_BLOCK, LANES)).reshape(ATTN_KV_HEADS, ATTN_GROUP * ATTN_BLOCK, LANES)
            x = _even_mixer(x, nw, ap_w_in_b, sink_rep, pool_w[i].astype(BF16), _row(pool_scale[i]),
                            ap_w_out_b, i, tile=even_tile)
        else:
            wdt = ssd_w_in[i][:, inner + conv_dim:].astype(BF16)
            x = _odd_mixer(x, nw, ssd_w_in_b, lane_pad(wdt), wdt.T, ssd_conv_w[i].astype(F32), _row(ssd_conv_b[i]),
                           lane_pad(_row(ssd_dt_bias[i])), _col(ssd_dt_bias[i]),
                           lane_pad(_row(ssd_A_log[i])), _col(ssd_A_log[i]),
                           _row(jnp.repeat(ssd_D[i], SSD_HEAD_DIM)), _row(ssd_norm_w[i]), expand,
                           ssd_w_out_b, i, tile=odd_tile)
        final = layer == depth - 1
        x = _ffn(x.reshape(b * s, d), _row(ffn_norm_w[layer]), w_gate_b, w_up_b, w_down_b, _row(final_norm_w),
                 layer, final=final, tile=ffn_tile).reshape(b, s, d)
    return x
```

```python
import functools

import jax
import jax.numpy as jnp
from jax import lax
from jax.experimental import pallas as pl
from jax.experimental.pallas import tpu as pltpu

F32 = jnp.float32
BF16 = jnp.bfloat16

EPS = 1e-5
LANES = 128
PIECE_COLS = 256
ATTN_HEADS = 8
ATTN_KV_HEADS = 2
HEAD_DIM = 64
ATTN_BLOCK = 128
ATTN_GROUP = ATTN_HEADS // ATTN_KV_HEADS
ATTN_WIDTH = ATTN_HEADS * HEAD_DIM
KV_WIDTH = ATTN_KV_HEADS * HEAD_DIM
POOL_WINDOWS = (2, 4, 8, 16)
POOL_GROUP_DIM = 128
POOL_WIDTH = len(POOL_WINDOWS) * POOL_GROUP_DIM
POOL_CARRY = 16
SSD_HEAD_DIM = 64
SSD_GROUPS = 4
SSD_STATE = 128
SSD_CONV = 4
SSD_CHUNK = 128
CONV_CARRY = 8
NEG = -1e30

VMEM_LIMIT = 56 * 1024 * 1024
CAST_BLOCK_BYTES = 6 * 1024 * 1024

EVEN_TILE = 512
ODD_TILE = 256
FFN_TILE = 512


def _dot(a, b):
    return jnp.dot(a, b, preferred_element_type=F32)


def _dot_nt(a, b):
    return lax.dot_general(a, b, (((1,), (1,)), ((), ())), preferred_element_type=F32)


def _rmsnorm(x, w):
    ms = jnp.mean(x * x, axis=-1, keepdims=True)
    return x * lax.rsqrt(ms + EPS) * w


def _silu(x):
    h = 0.5 * x
    return h + h * jnp.tanh(h)


def _split3(v):
    hi = v.astype(BF16)
    r = v - hi.astype(F32)
    mid = r.astype(BF16)
    lo = (r - mid.astype(F32)).astype(BF16)
    return hi, mid, lo


def _const_spec(shape):
    nd = len(shape)
    return pl.BlockSpec(shape, lambda *_: (0,) * nd, pipeline_mode=pl.Buffered(1))


def _layer_spec(w, layer):
    return pl.BlockSpec((None,) + w.shape[1:], lambda *_: (layer, 0, 0), pipeline_mode=pl.Buffered(1))


def _cast_kernel(x_ref, o_ref, *, cols):
    aligned = cols // LANES * LANES
    o_ref[0, :, :aligned] = x_ref[0, :, :aligned].astype(BF16)
    if aligned < cols:
        o_ref[0, :, aligned:] = jnp.zeros((o_ref.shape[1], o_ref.shape[2] - aligned), BF16)
        o_ref[0, :, aligned:cols] = x_ref[0, :, aligned:].astype(BF16)


def _to_bf16(w):
    nl, r, c = w.shape
    c_out = -(-c // LANES) * LANES
    rb = r
    while rb * c * 4 > CAST_BLOCK_BYTES and rb % 32 == 0:
        rb //= 2
    return pl.pallas_call(
        functools.partial(_cast_kernel, cols=c),
        grid=(nl, r // rb),
        in_specs=[pl.BlockSpec((1, rb, c), lambda i, j: (i, j, 0))],
        out_specs=pl.BlockSpec((1, rb, c_out), lambda i, j: (i, j, 0)),
        out_shape=jax.ShapeDtypeStruct((nl, r, c_out), BF16),
        compiler_params=pltpu.CompilerParams(
            dimension_semantics=("arbitrary", "arbitrary"), vmem_limit_bytes=VMEM_LIMIT),
        name="cast_bf16",
    )(w)


def _ffn_kernel(x_ref, nw_ref, wg_ref, wu_ref, wd_ref, fw_ref, o_ref, *, final):
    x = x_ref[...]
    h = _rmsnorm(x, nw_ref[...]).astype(BF16)
    g = _dot(h, wg_ref[...])
    u = _dot(h, wu_ref[...])
    a = (_silu(g) * u).astype(BF16)
    y = x + _dot(a, wd_ref[...])
    if final:
        y = _rmsnorm(y, fw_ref[...])
    o_ref[...] = y


def _ffn(x2d, nw, wg, wu, wd, fw, layer, *, final, tile):
    n, d = x2d.shape
    return pl.pallas_call(
        functools.partial(_ffn_kernel, final=final),
        grid=(n // tile,),
        in_specs=[
            pl.BlockSpec((tile, d), lambda i: (i, 0)),
            _const_spec((1, d)),
            _layer_spec(wg, layer),
            _layer_spec(wu, layer),
            _layer_spec(wd, layer),
            _const_spec((1, d)),
        ],
        out_specs=pl.BlockSpec((tile, d), lambda i: (i, 0)),
        out_shape=jax.ShapeDtypeStruct((n, d), F32),
        compiler_params=pltpu.CompilerParams(
            dimension_semantics=("arbitrary",), vmem_limit_bytes=VMEM_LIMIT),
        name="ffn",
    )(x2d, nw, wg, wu, wd, fw)


def _even_kernel(x_ref, nw_ref, win_ref, sink_ref, pw_ref, ps_ref, wout_ref, o_ref,
                 kprev_ref, vprev_ref, pcarry_ref, attn_ref, *, tile):
    t = pl.program_id(1)
    nblk = tile // ATTN_BLOCK
    blk = ATTN_BLOCK

    @pl.when(t == 0)
    def _():
        kprev_ref[...] = jnp.zeros_like(kprev_ref)
        vprev_ref[...] = jnp.zeros_like(vprev_ref)
        pcarry_ref[...] = jnp.zeros_like(pcarry_ref)

    x = x_ref[0]
    h = _rmsnorm(x, nw_ref[...]).astype(BF16)
    proj = _dot(h, win_ref[...])
    q = proj[:, :ATTN_WIDTH] * (HEAD_DIM ** -0.5)
    k = proj[:, ATTN_WIDTH:ATTN_WIDTH + KV_WIDTH]
    v = proj[:, ATTN_WIDTH + KV_WIDTH:ATTN_WIDTH + 2 * KV_WIDTH]
    p = proj[:, ATTN_WIDTH + 2 * KV_WIDTH:]
    q_is_even = lax.broadcasted_iota(jnp.int32, q.shape, 1) % LANES < HEAD_DIM
    q_par = (jnp.where(q_is_even, q, 0.0).astype(BF16), jnp.where(q_is_even, 0.0, q).astype(BF16))

    lane = lax.broadcasted_iota(jnp.int32, (tile, KV_WIDTH), 1)
    lo_half = lane < HEAD_DIM

    def both_halves(a):
        sw = pltpu.roll(a, HEAD_DIM, 1)
        return (jnp.where(lo_half, a, sw).astype(BF16), jnp.where(lo_half, sw, a).astype(BF16))

    k2 = both_halves(k)
    v2 = both_halves(v)
    kfull = [jnp.concatenate([kprev_ref[hh], k2[hh]], axis=0) for hh in range(ATTN_KV_HEADS)]
    vfull = [jnp.concatenate([vprev_ref[hh], v2[hh]], axis=0) for hh in range(ATTN_KV_HEADS)]
    for hh in range(ATTN_KV_HEADS):
        kprev_ref[hh] = k2[hh][tile - blk:]
        vprev_ref[hh] = v2[hh][tile - blk:]

    rows = ATTN_GROUP * blk
    qi = lax.broadcasted_iota(jnp.int32, (rows, 2 * blk), 0) % blk
    kj = lax.broadcasted_iota(jnp.int32, (rows, 2 * blk), 1)
    band = (kj > qi) & (kj <= qi + blk)
    first_key = jnp.where(t == 0, blk, 0)
    qlane = lax.broadcasted_iota(jnp.int32, (blk, LANES), 1)
    q_lo = qlane < HEAD_DIM
    ones = jnp.ones((2 * blk, LANES), BF16)

    for j in range(nblk):
        valid = band & (kj >= first_key) if j == 0 else band
        for hh in range(ATTN_KV_HEADS):
            parts = []
            for g in range(ATTN_GROUP):
                c0 = (hh * ATTN_GROUP + g) // 2 * LANES
                parts.append(q_par[g % 2][j * blk:(j + 1) * blk, c0:c0 + LANES])
            qst = jnp.concatenate(parts, axis=0)
            kb = kfull[hh][j * blk:(j + 2) * blk]
            vb = jnp.concatenate([vfull[hh][j * blk:(j + 2) * blk], ones], axis=1)
            s = _dot_nt(qst, kb)
            s = jnp.where(valid, s, NEG)
            sink = sink_ref[hh]
            m = jnp.maximum(jnp.max(s, axis=-1, keepdims=True), sink)
            e = jnp.exp(s - jnp.concatenate([m, m], axis=1))
            r = _dot(e.astype(BF16), vb)
            den = r[:, LANES:] + jnp.exp(sink - m)
            o = r[:, :LANES] * (1.0 / den)
            for a in range(ATTN_GROUP // 2):
                slab = jnp.where(q_lo, o[(2 * a) * blk:(2 * a + 1) * blk],
                                 o[(2 * a + 1) * blk:(2 * a + 2) * blk])
                c0 = (hh * ATTN_GROUP // 2 + a) * LANES
                attn_ref[j * blk:(j + 1) * blk, c0:c0 + LANES] = slab.astype(BF16)

    pe = jnp.concatenate([pcarry_ref[...], p], axis=0)
    pcarry_ref[...] = p[tile - POOL_CARRY:]
    cnt = (t * tile + 1 + lax.broadcasted_iota(jnp.int32, (tile, POOL_GROUP_DIM), 0)).astype(F32)
    pooled = []
    for gi, w in enumerate(POOL_WINDOWS):
        u = pe[:, gi * POOL_GROUP_DIM:(gi + 1) * POOL_GROUP_DIM]
        sm = u
        sh = 1
        while sh < w:
            sm = sm + pltpu.roll(sm, sh, 0)
            sh *= 2
        mean = sm[POOL_CARRY:] / jnp.minimum(cnt, float(w))
        d = (mean - u[POOL_CARRY:]).astype(BF16)
        yg = _dot(d, pw_ref[gi]) * ps_ref[:, gi * POOL_GROUP_DIM:(gi + 1) * POOL_GROUP_DIM]
        pooled.append(yg.astype(BF16))
    mixed = jnp.concatenate(pooled, axis=1)
    o_ref[0] = (x + _dot(attn_ref[...], wout_ref[:ATTN_WIDTH, :])
                + _dot(mixed, wout_ref[ATTN_WIDTH:, :]))


def _even_mixer(x, nw, w_in, sink_rep, pw, ps, w_out, layer, *, tile):
    b, s, d = x.shape
    return pl.pallas_call(
        functools.partial(_even_kernel, tile=tile),
        grid=(b, s // tile),
        in_specs=[
            pl.BlockSpec((1, tile, d), lambda i, j: (i, j, 0)),
            _const_spec(nw.shape),
            _layer_spec(w_in, layer),
            _const_spec(sink_rep.shape),
            _const_spec(pw.shape),
            _const_spec(ps.shape),
            _layer_spec(w_out, layer),
        ],
        out_specs=pl.BlockSpec((1, tile, d), lambda i, j: (i, j, 0)),
        out_shape=jax.ShapeDtypeStruct((b, s, d), F32),
        scratch_shapes=[
            pltpu.VMEM((ATTN_KV_HEADS, ATTN_BLOCK, KV_WIDTH), BF16),
            pltpu.VMEM((ATTN_KV_HEADS, ATTN_BLOCK, KV_WIDTH), BF16),
            pltpu.VMEM((POOL_CARRY, POOL_WIDTH), F32),
            pltpu.VMEM((tile, ATTN_WIDTH), BF16),
        ],
        compiler_params=pltpu.CompilerParams(
            dimension_semantics=("arbitrary", "arbitrary"), vmem_limit_bytes=VMEM_LIMIT),
        name="even_mixer",
    )(x, nw, w_in, sink_rep, pw, ps, w_out)


def _softplus(x):
    return jnp.maximum(x, 0.0) + jnp.log(1.0 + jnp.exp(-jnp.abs(x)))


def _odd_kernel(x_ref, nw_ref, win_ref, cw_ref, cb_ref, dtb_ref, dtbT_ref,
                alog_ref, alogT_ref, dexp_ref, gnw_ref, expand_ref, wout_ref, o_ref,
                state_ref, ccarry_ref, act_ref, z_ref, y_ref, *, tile, inner, heads):
    t = pl.program_id(1)
    L = SSD_CHUNK
    nchunk = tile // L
    gn = SSD_GROUPS * SSD_STATE
    conv_dim = inner + 2 * gn
    heads_per_group = heads // SSD_GROUPS
    pairs_per_group = heads_per_group // 2
    gw = heads_per_group * SSD_HEAD_DIM

    @pl.when(t == 0)
    def _():
        state_ref[...] = jnp.zeros_like(state_ref)
        ccarry_ref[...] = jnp.zeros_like(ccarry_ref)

    x = x_ref[0]
    hn = _rmsnorm(x, nw_ref[...]).astype(BF16)

    def conv_cols(c0, width):
        xb = _dot(hn, win_ref[:, inner + c0:inner + c0 + width])
        xe = jnp.concatenate([ccarry_ref[:, c0:c0 + width], xb], axis=0)
        ccarry_ref[:, c0:c0 + width] = xb[tile - CONV_CARRY:]
        acc = xe * cw_ref[SSD_CONV - 1:SSD_CONV, c0:c0 + width]
        for kk in range(1, SSD_CONV):
            acc = acc + pltpu.roll(xe, kk, 0) * cw_ref[SSD_CONV - 1 - kk:SSD_CONV - kk, c0:c0 + width]
        act_ref[:, c0:c0 + width] = _silu(acc[CONV_CARRY:] + cb_ref[:, c0:c0 + width])

    def z_cols(c0, width):
        z_ref[:, c0:c0 + width] = _dot(hn, win_ref[:, c0:c0 + width])

    def project_group_items(g):
        items = []
        for c0 in range(g * gw, (g + 1) * gw, PIECE_COLS):
            items.append(functools.partial(conv_cols, c0, PIECE_COLS))
            items.append(functools.partial(z_cols, c0, PIECE_COLS))
        return items

    def finish_group_items(g):
        g0 = g * gw
        cell = []

        def gate():
            yg = y_ref[:, g0:g0 + gw] + act_ref[:, g0:g0 + gw] * dexp_ref[:, g0:g0 + gw]
            yg = yg * _silu(z_ref[:, g0:g0 + gw])
            ms = jnp.mean(yg * yg, axis=-1, keepdims=True)
            cell.append((yg * lax.rsqrt(ms + EPS) * gnw_ref[:, g0:g0 + gw]).astype(BF16))

        def out_proj():
            part = _dot(cell.pop(), wout_ref[g0:g0 + gw, :])
            if g == 0:
                o_ref[0] = x + part
            else:
                o_ref[0] = o_ref[0] + part

        return [gate, out_proj]

    for c0 in range(inner, conv_dim, PIECE_COLS):
        conv_cols(c0, PIECE_COLS)
    dtr = _dot(hn, win_ref[:, inner + conv_dim:])
    dtrT = dtr.T[:heads]

    a_neg = -jnp.exp(alog_ref[...])
    a_negT = -jnp.exp(alogT_ref[...])

    li = lax.broadcasted_iota(jnp.int32, (L, L), 0)
    si = lax.broadcasted_iota(jnp.int32, (L, L), 1)
    causal = si <= li
    tril = jnp.where(causal, 1.0, 0.0).astype(BF16)
    triu = jnp.where(li <= si, 1.0, 0.0).astype(BF16)
    plane = lax.broadcasted_iota(jnp.int32, (L, LANES), 1)
    p_lo = plane < SSD_HEAD_DIM

    chunks = []
    for c in range(nchunk):
        r0 = c * L
        dt = _softplus(dtr[r0:r0 + L] + dtb_ref[...])
        dtT = _softplus(dtrT[:, r0:r0 + L] + dtbT_ref[...])
        a_cs = sum(_dot(tril, part) for part in _split3(dt * a_neg))
        a_csT = sum(_dot(part, triu) for part in _split3(dtT * a_negT))
        a_last = a_cs[L - 1:L, :]
        a_lastT = a_csT[:, L - 1:L]
        a_last16 = jnp.broadcast_to(a_last, (16, a_last.shape[1]))
        chunks.append(dict(
            a_cs=a_cs,
            wT=dtT * jnp.exp(a_lastT - a_csT),
            a_shiftT=a_csT - jnp.log(dtT),
            ea=jnp.exp(a_cs),
            cd_row=jnp.exp(sum(_dot(part, expand_ref[...]) for part in _split3(a_last16))[0:1]),
        ))

    for item in project_group_items(0):
        item()
    pending = []
    for g in range(SSD_GROUPS):
        if g + 1 < SSD_GROUPS:
            pending.extend(project_group_items(g + 1))
        g0 = g * gw
        for c, ck in enumerate(chunks):
            r0 = c * L
            bg = act_ref[r0:r0 + L, inner + g * SSD_STATE:inner + (g + 1) * SSD_STATE]
            cg = act_ref[r0:r0 + L, inner + gn + g * SSD_STATE:inner + gn + (g + 1) * SSD_STATE].astype(BF16)
            cb = _dot_nt(cg, bg.astype(BF16))
            bgT = bg.T
            y_off = _dot(cg, state_ref[:, g0:g0 + gw].astype(BF16))
            for pr in range(pairs_per_group):
                h0 = g * heads_per_group + 2 * pr
                c0 = h0 * SSD_HEAD_DIM
                xp = act_ref[r0:r0 + L, c0:c0 + LANES]
                w_a = jnp.concatenate([jnp.where(p_lo, xp, 0.0), jnp.where(p_lo, 0.0, xp)], axis=0).astype(BF16)
                m_parts, b_parts, e_parts = [], [], []
                for hd in (h0, h0 + 1):
                    acol = jnp.broadcast_to(ck["a_cs"][:, hd:hd + 1], (L, L))
                    dec = jnp.exp(jnp.where(causal, acol - ck["a_shiftT"][hd:hd + 1, :], NEG))
                    m_parts.append((cb * dec).astype(BF16))
                    b_parts.append((bgT * ck["wT"][hd:hd + 1, :]).astype(BF16))
                    e_parts.append(jnp.broadcast_to(ck["ea"][:, hd:hd + 1], (L, LANES)))
                lhs = jnp.concatenate([jnp.concatenate(m_parts, axis=1),
                                       jnp.concatenate(b_parts, axis=1)], axis=0)
                ra = _dot(lhs, w_a)
                ea_pair = jnp.where(p_lo, e_parts[0], e_parts[1])
                y_ref[r0:r0 + L, c0:c0 + LANES] = ra[:L] + y_off[:, c0 - g0:c0 - g0 + LANES] * ea_pair
                state_ref[:, c0:c0 + LANES] = (state_ref[:, c0:c0 + LANES] * ck["cd_row"][:, c0:c0 + LANES]
                                               + ra[L:])
                if pending:
                    pending.pop(0)()
        while pending:
            pending.pop(0)()
        pending.extend(finish_group_items(g))
    while pending:
        pending.pop(0)()


def _odd_mixer(x, nw, w_in, cw, cb, dtb, dtbT, alog, alogT, dexp, gnw, expand, w_out, layer, *, tile):
    b, s, d = x.shape
    heads = dtbT.shape[0]
    inner = heads * SSD_HEAD_DIM
    conv_dim = inner + 2 * SSD_GROUPS * SSD_STATE
    small = (cw, cb, dtb, dtbT, alog, alogT, dexp, gnw, expand)
    return pl.pallas_call(
        functools.partial(_odd_kernel, tile=tile, inner=inner, heads=heads),
        grid=(b, s // tile),
        in_specs=[pl.BlockSpec((1, tile, d), lambda i, j: (i, j, 0)), _const_spec(nw.shape),
                  _layer_spec(w_in, layer)]
        + [_const_spec(c.shape) for c in small] + [_layer_spec(w_out, layer)],
        out_specs=pl.BlockSpec((1, tile, d), lambda i, j: (i, j, 0)),
        out_shape=jax.ShapeDtypeStruct((b, s, d), F32),
        scratch_shapes=[
            pltpu.VMEM((SSD_STATE, inner), F32),
            pltpu.VMEM((CONV_CARRY, conv_dim), F32),
            pltpu.VMEM((tile, conv_dim), F32),
            pltpu.VMEM((tile, inner), F32),
            pltpu.VMEM((tile, inner), F32),
        ],
        compiler_params=pltpu.CompilerParams(
            dimension_semantics=("arbitrary", "arbitrary"), vmem_limit_bytes=VMEM_LIMIT),
        name="odd_mixer",
    )(x, nw, w_in, *small, w_out)


def _row(v):
    return v.reshape(1, -1).astype(F32)


def _col(v):
    return v.reshape(-1, 1).astype(F32)


def kernel(x, mix_norm_w, ap_w_in, ap_sinks, pool_w, pool_scale, ap_w_out, ssd_w_in, ssd_conv_w, ssd_conv_b, ssd_dt_bias, ssd_A_log, ssd_D, ssd_norm_w, ssd_w_out, ffn_norm_w, w_gate, w_up, w_down, final_norm_w):
    b, s, d = x.shape
    depth = mix_norm_w.shape[0]
    heads = ssd_dt_bias.shape[1]
    inner = heads * SSD_HEAD_DIM
    conv_dim = inner + 2 * SSD_GROUPS * SSD_STATE
    even_tile = min(EVEN_TILE, s)
    odd_tile = min(ODD_TILE, s)
    ffn_tile = min(FFN_TILE, b * s)

    ap_w_in_b, ap_w_out_b = _to_bf16(ap_w_in), _to_bf16(ap_w_out)
    ssd_w_in_b, ssd_w_out_b = _to_bf16(ssd_w_in), _to_bf16(ssd_w_out)
    w_gate_b, w_up_b, w_down_b = _to_bf16(w_gate), _to_bf16(w_up), _to_bf16(w_down)

    def lane_pad(a):
        return jnp.pad(a, ((0, 0), (0, LANES - heads)))

    expand = jnp.pad(jnp.repeat(jnp.eye(heads, dtype=BF16), SSD_HEAD_DIM, axis=1),
                     ((0, LANES - heads), (0, 0)))

    for layer in range(depth):
        i = layer // 2
        nw = _row(mix_norm_w[layer])
        if layer % 2 == 0:
            sink_rep = jnp.broadcast_to(
                ap_sinks[i].astype(F32).reshape(ATTN_KV_HEADS, ATTN_GROUP, 1, 1),
                (ATTN_KV_HEADS, ATTN_GROUP, ATTN_BLOCK, LANES)).reshape(ATTN_KV_HEADS, ATTN_GROUP * ATTN_BLOCK, LANES)
            x = _even_mixer(x, nw, ap_w_in_b, sink_rep, pool_w[i].astype(BF16), _row(pool_scale[i]),
                            ap_w_out_b, i, tile=even_tile)
        else:
            x = _odd_mixer(x, nw, ssd_w_in_b, ssd_conv_w[i].astype(F32), _row(ssd_conv_b[i]),
                           lane_pad(_row(ssd_dt_bias[i])), _col(ssd_dt_bias[i]),
                           lane_pad(_row(ssd_A_log[i])), _col(ssd_A_log[i]),
                           _row(jnp.repeat(ssd_D[i], SSD_HEAD_DIM)), _row(ssd_norm_w[i]), expand,
                           ssd_w_out_b, i, tile=odd_tile)
        final = layer == depth - 1
        x = _ffn(x.reshape(b * s, d), _row(ffn_norm_w[layer]), w_gate_b, w_up_b, w_down_b, _row(final_norm_w),
                 layer, final=final, tile=ffn_tile).reshape(b, s, d)
    return x
```

```python
import functools
from typing import NamedTuple

import jax
import jax.numpy as jnp
from jax import lax
from jax.experimental import pallas as pl
from jax.experimental.pallas import tpu as pltpu

F32 = jnp.float32
BF16 = jnp.bfloat16

EPS = 1e-5
LANES = 128
BF16_SUBLANES = 16
PIECE_COLS = 256
ATTN_HEADS = 8
ATTN_KV_HEADS = 2
HEAD_DIM = 64
ATTN_BLOCK = 128
ATTN_GROUP = ATTN_HEADS // ATTN_KV_HEADS
ATTN_WIDTH = ATTN_HEADS * HEAD_DIM
KV_WIDTH = ATTN_KV_HEADS * HEAD_DIM
POOL_WINDOWS = (2, 4, 8, 16)
POOL_GROUP_DIM = 128
POOL_WIDTH = len(POOL_WINDOWS) * POOL_GROUP_DIM
POOL_CARRY = 16
SSD_HEAD_DIM = 64
SSD_GROUPS = 4
SSD_STATE = 128
SSD_CONV = 4
SSD_CHUNK = 128
CONV_CARRY = 8
NEG = -1e30

VMEM_LIMIT = 56 * 1024 * 1024
CAST_BLOCK_BYTES = 6 * 1024 * 1024

EVEN_TILE = 512
ODD_TILE = 256
FFN_TILE = 512


def _dot(a, b):
    return jnp.dot(a, b, preferred_element_type=F32)


def _dot_nt(a, b):
    return lax.dot_general(a, b, (((1,), (1,)), ((), ())), preferred_element_type=F32)


def _rmsnorm(x, w):
    ms = jnp.mean(x * x, axis=-1, keepdims=True)
    return x * lax.rsqrt(ms + EPS) * w


def _silu(x):
    h = 0.5 * x
    return h + h * jnp.tanh(h)


def _split3(v):
    hi = v.astype(BF16)
    r = v - hi.astype(F32)
    mid = r.astype(BF16)
    lo = (r - mid.astype(F32)).astype(BF16)
    return hi, mid, lo


def _const_spec(shape):
    nd = len(shape)
    return pl.BlockSpec(shape, lambda *_: (0,) * nd, pipeline_mode=pl.Buffered(1))


def _layer_spec(w, layer):
    return pl.BlockSpec((None,) + w.shape[1:], lambda *_: (layer, 0, 0), pipeline_mode=pl.Buffered(1))


def _cast_kernel(x_ref, o_ref, *, cols):
    aligned = cols // LANES * LANES
    o_ref[0, :, :aligned] = x_ref[0, :, :aligned].astype(BF16)
    if aligned < cols:
        o_ref[0, :, aligned:] = jnp.zeros((o_ref.shape[1], o_ref.shape[2] - aligned), BF16)
        o_ref[0, :, aligned:cols] = x_ref[0, :, aligned:].astype(BF16)


def _to_bf16(w):
    nl, r, c = w.shape
    c_out = -(-c // LANES) * LANES
    rb = r
    while rb * c * 4 > CAST_BLOCK_BYTES and rb % 32 == 0:
        rb //= 2
    return pl.pallas_call(
        functools.partial(_cast_kernel, cols=c),
        grid=(nl, r // rb),
        in_specs=[pl.BlockSpec((1, rb, c), lambda i, j: (i, j, 0))],
        out_specs=pl.BlockSpec((1, rb, c_out), lambda i, j: (i, j, 0)),
        out_shape=jax.ShapeDtypeStruct((nl, r, c_out), BF16),
        compiler_params=pltpu.CompilerParams(
            dimension_semantics=("arbitrary", "arbitrary"), vmem_limit_bytes=VMEM_LIMIT),
        name="cast_bf16",
    )(w)


class _Cast(NamedTuple):
    w: jax.Array
    layer: int
    transpose: bool = False


def _cast_plan(cast, nsteps):
    if cast.transpose:
        _, c, r = cast.w.shape
        nstrips = max(k for k in range(1, nsteps + 1) if nsteps % k == 0 and (r // LANES) % k == 0)
        c_out = -(-c // LANES) * LANES
        plan = ((None, c, r // nstrips), lambda s: (cast.layer, 0, s), (r // nstrips, c_out), (r, c_out))
    else:
        _, r, c = cast.w.shape
        nstrips = max(k for k in range(1, nsteps + 1)
                      if nsteps % k == 0 and r % k == 0 and (r // k) % BF16_SUBLANES == 0)
        rb = r // nstrips
        plan = ((None, rb, c), lambda s: (cast.layer, s, 0), (rb, c), (r, c))
    assert nsteps % nstrips == 0
    return plan + (nsteps // nstrips,)


def _emit_cast(transpose, reps, step, i_ref, o_ref):
    if not transpose:
        o_ref[...] = i_ref[...].astype(BF16)
        return

    @pl.when(step % reps == 0)
    def _():
        c, width = i_ref.shape
        aligned = c // LANES * LANES
        xt = i_ref[...]
        parts = [xt[:aligned].T]
        if aligned < c:
            tail = jnp.concatenate([xt[aligned:], jnp.zeros((LANES - (c - aligned), width), F32)], axis=0)
            parts.append(tail.T)
        o_ref[...] = jnp.concatenate(parts, axis=1).astype(BF16)


def _fused_kernel(*refs, body, n_in, cast_meta, grid):
    n_c = len(cast_meta)
    ins, cast_ins = refs[:n_in], refs[n_in:n_in + n_c]
    out, cast_outs = refs[n_in + n_c], refs[n_in + n_c + 1:n_in + 2 * n_c + 1]
    scratch = refs[n_in + 2 * n_c + 1:]
    body(*ins, out, *scratch)
    step = pl.program_id(0)
    for axis in range(1, len(grid)):
        step = step * grid[axis] + pl.program_id(axis)
    for (transpose, reps), i_ref, o_ref in zip(cast_meta, cast_ins, cast_outs):
        _emit_cast(transpose, reps, step, i_ref, o_ref)


def _fused_call(body, *, name, grid, in_specs, operands, out_spec, out_shape, scratch_shapes=(), casts=()):
    nsteps = 1
    for g in grid:
        nsteps *= g

    def flat(idx):
        s = idx[0]
        for axis in range(1, len(grid)):
            s = s * grid[axis] + idx[axis]
        return s

    cast_in_specs, cast_out_specs, cast_out_shapes, cast_meta = [], [], [], []
    for cast in casts:
        in_block, in_index, out_block, o_shape, reps = _cast_plan(cast, nsteps)
        cast_in_specs.append(pl.BlockSpec(in_block, lambda *g, f=in_index, r=reps: f(flat(g) // r)))
        cast_out_specs.append(pl.BlockSpec(out_block, lambda *g, r=reps: (flat(g) // r, 0)))
        cast_out_shapes.append(jax.ShapeDtypeStruct(o_shape, BF16))
        cast_meta.append((cast.transpose, reps))
    outs = pl.pallas_call(
        functools.partial(_fused_kernel, body=body, n_in=len(in_specs), cast_meta=tuple(cast_meta), grid=grid),
        grid=grid,
        in_specs=list(in_specs) + cast_in_specs,
        out_specs=[out_spec] + cast_out_specs,
        out_shape=[out_shape] + cast_out_shapes,
        scratch_shapes=list(scratch_shapes),
        compiler_params=pltpu.CompilerParams(
            dimension_semantics=("arbitrary",) * len(grid), vmem_limit_bytes=VMEM_LIMIT),
        name=name,
    )(*operands, *[c.w for c in casts])
    return outs[0], list(outs[1:])


def _weight_spec(w, layer):
    return _const_spec(w.shape) if w.ndim == 2 else _layer_spec(w, layer)


def _ffn_kernel(x_ref, nw_ref, wg_ref, wu_ref, wd_ref, fw_ref, o_ref, *, final):
    x = x_ref[...]
    h = _rmsnorm(x, nw_ref[...]).astype(BF16)
    g = _dot(h, wg_ref[...])
    u = _dot(h, wu_ref[...])
    a = (_silu(g) * u).astype(BF16)
    y = x + _dot(a, wd_ref[...])
    if final:
        y = _rmsnorm(y, fw_ref[...])
    o_ref[...] = y


def _ffn(x2d, nw, wg, wu, wd, fw, *, final, tile, casts=()):
    n, d = x2d.shape
    return _fused_call(
        functools.partial(_ffn_kernel, final=final),
        name="ffn",
        grid=(n // tile,),
        in_specs=[
            pl.BlockSpec((tile, d), lambda i: (i, 0)),
            _const_spec((1, d)),
            _const_spec(wg.shape),
            _const_spec(wu.shape),
            _const_spec(wd.shape),
            _const_spec((1, d)),
        ],
        operands=(x2d, nw, wg, wu, wd, fw),
        out_spec=pl.BlockSpec((tile, d), lambda i: (i, 0)),
        out_shape=jax.ShapeDtypeStruct((n, d), F32),
        casts=casts,
    )


def _even_kernel(x_ref, nw_ref, win_ref, sink_ref, pw_ref, ps_ref, wout_ref, o_ref,
                 kprev_ref, vprev_ref, pcarry_ref, attn_ref, *, tile):
    t = pl.program_id(1)
    nblk = tile // ATTN_BLOCK
    blk = ATTN_BLOCK

    @pl.when(t == 0)
    def _():
        kprev_ref[...] = jnp.zeros_like(kprev_ref)
        vprev_ref[...] = jnp.zeros_like(vprev_ref)
        pcarry_ref[...] = jnp.zeros_like(pcarry_ref)

    x = x_ref[0]
    h = _rmsnorm(x, nw_ref[...]).astype(BF16)
    proj = _dot(h, win_ref[...])
    q = proj[:, :ATTN_WIDTH] * (HEAD_DIM ** -0.5)
    k = proj[:, ATTN_WIDTH:ATTN_WIDTH + KV_WIDTH]
    v = proj[:, ATTN_WIDTH + KV_WIDTH:ATTN_WIDTH + 2 * KV_WIDTH]
    p = proj[:, ATTN_WIDTH + 2 * KV_WIDTH:]
    q_is_even = lax.broadcasted_iota(jnp.int32, q.shape, 1) % LANES < HEAD_DIM
    q_par = (jnp.where(q_is_even, q, 0.0).astype(BF16), jnp.where(q_is_even, 0.0, q).astype(BF16))

    lane = lax.broadcasted_iota(jnp.int32, (tile, KV_WIDTH), 1)
    lo_half = lane < HEAD_DIM

    def both_halves(a):
        sw = pltpu.roll(a, HEAD_DIM, 1)
        return (jnp.where(lo_half, a, sw).astype(BF16), jnp.where(lo_half, sw, a).astype(BF16))

    k2 = both_halves(k)
    v2 = both_halves(v)
    kfull = [jnp.concatenate([kprev_ref[hh], k2[hh]], axis=0) for hh in range(ATTN_KV_HEADS)]
    vfull = [jnp.concatenate([vprev_ref[hh], v2[hh]], axis=0) for hh in range(ATTN_KV_HEADS)]
    for hh in range(ATTN_KV_HEADS):
        kprev_ref[hh] = k2[hh][tile - blk:]
        vprev_ref[hh] = v2[hh][tile - blk:]

    rows = ATTN_GROUP * blk
    qi = lax.broadcasted_iota(jnp.int32, (rows, 2 * blk), 0) % blk
    kj = lax.broadcasted_iota(jnp.int32, (rows, 2 * blk), 1)
    band = (kj > qi) & (kj <= qi + blk)
    first_key = jnp.where(t == 0, blk, 0)
    qlane = lax.broadcasted_iota(jnp.int32, (blk, LANES), 1)
    q_lo = qlane < HEAD_DIM
    ones = jnp.ones((2 * blk, LANES), BF16)

    for j in range(nblk):
        valid = band & (kj >= first_key) if j == 0 else band
        for hh in range(ATTN_KV_HEADS):
            parts = []
            for g in range(ATTN_GROUP):
                c0 = (hh * ATTN_GROUP + g) // 2 * LANES
                parts.append(q_par[g % 2][j * blk:(j + 1) * blk, c0:c0 + LANES])
            qst = jnp.concatenate(parts, axis=0)
            kb = kfull[hh][j * blk:(j + 2) * blk]
            vb = jnp.concatenate([vfull[hh][j * blk:(j + 2) * blk], ones], axis=1)
            s = _dot_nt(qst, kb)
            s = jnp.where(valid, s, NEG)
            sink = sink_ref[hh]
            m = jnp.maximum(jnp.max(s, axis=-1, keepdims=True), sink)
            e = jnp.exp(s - jnp.concatenate([m, m], axis=1))
            r = _dot(e.astype(BF16), vb)
            den = r[:, LANES:] + jnp.exp(sink - m)
            o = r[:, :LANES] * (1.0 / den)
            for a in range(ATTN_GROUP // 2):
                slab = jnp.where(q_lo, o[(2 * a) * blk:(2 * a + 1) * blk],
                                 o[(2 * a + 1) * blk:(2 * a + 2) * blk])
                c0 = (hh * ATTN_GROUP // 2 + a) * LANES
                attn_ref[j * blk:(j + 1) * blk, c0:c0 + LANES] = slab.astype(BF16)

    pe = jnp.concatenate([pcarry_ref[...], p], axis=0)
    pcarry_ref[...] = p[tile - POOL_CARRY:]
    cnt = (t * tile + 1 + lax.broadcasted_iota(jnp.int32, (tile, POOL_GROUP_DIM), 0)).astype(F32)
    pooled = []
    for gi, w in enumerate(POOL_WINDOWS):
        u = pe[:, gi * POOL_GROUP_DIM:(gi + 1) * POOL_GROUP_DIM]
        sm = u
        sh = 1
        while sh < w:
            sm = sm + pltpu.roll(sm, sh, 0)
            sh *= 2
        mean = sm[POOL_CARRY:] / jnp.minimum(cnt, float(w))
        d = (mean - u[POOL_CARRY:]).astype(BF16)
        yg = _dot(d, pw_ref[gi]) * ps_ref[:, gi * POOL_GROUP_DIM:(gi + 1) * POOL_GROUP_DIM]
        pooled.append(yg.astype(BF16))
    mixed = jnp.concatenate(pooled, axis=1)
    o_ref[0] = (x + _dot(attn_ref[...], wout_ref[:ATTN_WIDTH, :])
                + _dot(mixed, wout_ref[ATTN_WIDTH:, :]))


def _even_mixer(x, nw, w_in, sink_rep, pw, ps, w_out, layer, *, tile, casts=()):
    b, s, d = x.shape
    return _fused_call(
        functools.partial(_even_kernel, tile=tile),
        name="even_mixer",
        grid=(b, s // tile),
        in_specs=[
            pl.BlockSpec((1, tile, d), lambda i, j: (i, j, 0)),
            _const_spec(nw.shape),
            _weight_spec(w_in, layer),
            _const_spec(sink_rep.shape),
            _const_spec(pw.shape),
            _const_spec(ps.shape),
            _weight_spec(w_out, layer),
        ],
        operands=(x, nw, w_in, sink_rep, pw, ps, w_out),
        out_spec=pl.BlockSpec((1, tile, d), lambda i, j: (i, j, 0)),
        out_shape=jax.ShapeDtypeStruct((b, s, d), F32),
        scratch_shapes=[
            pltpu.VMEM((ATTN_KV_HEADS, ATTN_BLOCK, KV_WIDTH), BF16),
            pltpu.VMEM((ATTN_KV_HEADS, ATTN_BLOCK, KV_WIDTH), BF16),
            pltpu.VMEM((POOL_CARRY, POOL_WIDTH), F32),
            pltpu.VMEM((tile, ATTN_WIDTH), BF16),
        ],
        casts=casts,
    )


def _softplus(x):
    return jnp.maximum(x, 0.0) + jnp.log(1.0 + jnp.exp(-jnp.abs(x)))


def _odd_kernel(x_ref, nw_ref, win_ref, cw_ref, cb_ref, dtb_ref, dtbT_ref,
                alog_ref, alogT_ref, dexp_ref, gnw_ref, expand_ref, wout_ref, o_ref,
                state_ref, ccarry_ref, act_ref, z_ref, y_ref, *, tile, inner, heads):
    t = pl.program_id(1)
    L = SSD_CHUNK
    nchunk = tile // L
    gn = SSD_GROUPS * SSD_STATE
    conv_dim = inner + 2 * gn
    heads_per_group = heads // SSD_GROUPS
    pairs_per_group = heads_per_group // 2
    gw = heads_per_group * SSD_HEAD_DIM

    @pl.when(t == 0)
    def _():
        state_ref[...] = jnp.zeros_like(state_ref)
        ccarry_ref[...] = jnp.zeros_like(ccarry_ref)

    x = x_ref[0]
    hn = _rmsnorm(x, nw_ref[...]).astype(BF16)

    def conv_cols(c0, width):
        xb = _dot(hn, win_ref[:, inner + c0:inner + c0 + width])
        xe = jnp.concatenate([ccarry_ref[:, c0:c0 + width], xb], axis=0)
        ccarry_ref[:, c0:c0 + width] = xb[tile - CONV_CARRY:]
        acc = xe * cw_ref[SSD_CONV - 1:SSD_CONV, c0:c0 + width]
        for kk in range(1, SSD_CONV):
            acc = acc + pltpu.roll(xe, kk, 0) * cw_ref[SSD_CONV - 1 - kk:SSD_CONV - kk, c0:c0 + width]
        act_ref[:, c0:c0 + width] = _silu(acc[CONV_CARRY:] + cb_ref[:, c0:c0 + width])

    def z_cols(c0, width):
        z_ref[:, c0:c0 + width] = _dot(hn, win_ref[:, c0:c0 + width])

    def project_group_items(g):
        items = []
        for c0 in range(g * gw, (g + 1) * gw, PIECE_COLS):
            items.append(functools.partial(conv_cols, c0, PIECE_COLS))
            items.append(functools.partial(z_cols, c0, PIECE_COLS))
        return items

    def finish_group_items(g):
        g0 = g * gw
        cell = []

        def gate():
            yg = y_ref[:, g0:g0 + gw] + act_ref[:, g0:g0 + gw] * dexp_ref[:, g0:g0 + gw]
            yg = yg * _silu(z_ref[:, g0:g0 + gw])
            ms = jnp.mean(yg * yg, axis=-1, keepdims=True)
            cell.append((yg * lax.rsqrt(ms + EPS) * gnw_ref[:, g0:g0 + gw]).astype(BF16))

        def out_proj():
            part = _dot(cell.pop(), wout_ref[g0:g0 + gw, :])
            if g == 0:
                o_ref[0] = x + part
            else:
                o_ref[0] = o_ref[0] + part

        return [gate, out_proj]

    for c0 in range(inner, conv_dim, PIECE_COLS):
        conv_cols(c0, PIECE_COLS)
    dtr = _dot(hn, win_ref[:, inner + conv_dim:])
    dtrT = dtr.T[:heads]

    a_neg = -jnp.exp(alog_ref[...])
    a_negT = -jnp.exp(alogT_ref[...])

    li = lax.broadcasted_iota(jnp.int32, (L, L), 0)
    si = lax.broadcasted_iota(jnp.int32, (L, L), 1)
    causal = si <= li
    tril = jnp.where(causal, 1.0, 0.0).astype(BF16)
    triu = jnp.where(li <= si, 1.0, 0.0).astype(BF16)
    plane = lax.broadcasted_iota(jnp.int32, (L, LANES), 1)
    p_lo = plane < SSD_HEAD_DIM

    chunks = []
    for c in range(nchunk):
        r0 = c * L
        dt = _softplus(dtr[r0:r0 + L] + dtb_ref[...])
        dtT = _softplus(dtrT[:, r0:r0 + L] + dtbT_ref[...])
        a_cs = sum(_dot(tril, part) for part in _split3(dt * a_neg))
        a_csT = sum(_dot(part, triu) for part in _split3(dtT * a_negT))
        a_last = a_cs[L - 1:L, :]
        a_lastT = a_csT[:, L - 1:L]
        a_last16 = jnp.broadcast_to(a_last, (16, a_last.shape[1]))
        chunks.append(dict(
            a_cs=a_cs,
            wT=dtT * jnp.exp(a_lastT - a_csT),
            a_shiftT=a_csT - jnp.log(dtT),
            ea=jnp.exp(a_cs),
            cd_row=jnp.exp(sum(_dot(part, expand_ref[...]) for part in _split3(a_last16))[0:1]),
        ))

    for item in project_group_items(0):
        item()
    pending = []
    for g in range(SSD_GROUPS):
        if g + 1 < SSD_GROUPS:
            pending.extend(project_group_items(g + 1))
        g0 = g * gw
        for c, ck in enumerate(chunks):
            r0 = c * L
            bg = act_ref[r0:r0 + L, inner + g * SSD_STATE:inner + (g + 1) * SSD_STATE]
            cg = act_ref[r0:r0 + L, inner + gn + g * SSD_STATE:inner + gn + (g + 1) * SSD_STATE].astype(BF16)
            cb = _dot_nt(cg, bg.astype(BF16))
            bgT = bg.T
            y_off = _dot(cg, state_ref[:, g0:g0 + gw].astype(BF16))
            for pr in range(pairs_per_group):
                h0 = g * heads_per_group + 2 * pr
                c0 = h0 * SSD_HEAD_DIM
                xp = act_ref[r0:r0 + L, c0:c0 + LANES]
                w_a = jnp.concatenate([jnp.where(p_lo, xp, 0.0), jnp.where(p_lo, 0.0, xp)], axis=0).astype(BF16)
                m_parts, b_parts, e_parts = [], [], []
                for hd in (h0, h0 + 1):
                    acol = jnp.broadcast_to(ck["a_cs"][:, hd:hd + 1], (L, L))
                    dec = jnp.exp(jnp.where(causal, acol - ck["a_shiftT"][hd:hd + 1, :], NEG))
                    m_parts.append((cb * dec).astype(BF16))
                    b_parts.append((bgT * ck["wT"][hd:hd + 1, :]).astype(BF16))
                    e_parts.append(jnp.broadcast_to(ck["ea"][:, hd:hd + 1], (L, LANES)))
                lhs = jnp.concatenate([jnp.concatenate(m_parts, axis=1),
                                       jnp.concatenate(b_parts, axis=1)], axis=0)
                ra = _dot(lhs, w_a)
                ea_pair = jnp.where(p_lo, e_parts[0], e_parts[1])
                y_ref[r0:r0 + L, c0:c0 + LANES] = ra[:L] + y_off[:, c0 - g0:c0 - g0 + LANES] * ea_pair
                state_ref[:, c0:c0 + LANES] = (state_ref[:, c0:c0 + LANES] * ck["cd_row"][:, c0:c0 + LANES]
                                               + ra[L:])
                if pending:
                    pending.pop(0)()
        while pending:
            pending.pop(0)()
        pending.extend(finish_group_items(g))
    while pending:
        pending.pop(0)()


def _odd_mixer(x, nw, w_in, cw, cb, dtb, dtbT, alog, alogT, dexp, gnw, expand, w_out, *, tile, casts=()):
    b, s, d = x.shape
    heads = dtbT.shape[0]
    inner = heads * SSD_HEAD_DIM
    conv_dim = inner + 2 * SSD_GROUPS * SSD_STATE
    operands = (x, nw, w_in, cw, cb, dtb, dtbT, alog, alogT, dexp, gnw, expand, w_out)
    return _fused_call(
        functools.partial(_odd_kernel, tile=tile, inner=inner, heads=heads),
        name="odd_mixer",
        grid=(b, s // tile),
        in_specs=[pl.BlockSpec((1, tile, d), lambda i, j: (i, j, 0))]
        + [_const_spec(c.shape) for c in operands[1:]],
        operands=operands,
        out_spec=pl.BlockSpec((1, tile, d), lambda i, j: (i, j, 0)),
        out_shape=jax.ShapeDtypeStruct((b, s, d), F32),
        scratch_shapes=[
            pltpu.VMEM((SSD_STATE, inner), F32),
            pltpu.VMEM((CONV_CARRY, conv_dim), F32),
            pltpu.VMEM((tile, conv_dim), F32),
            pltpu.VMEM((tile, inner), F32),
            pltpu.VMEM((tile, inner), F32),
        ],
        casts=casts,
    )


def _row(v):
    return v.reshape(1, -1).astype(F32)


def _col(v):
    return v.reshape(-1, 1).astype(F32)


def kernel(x, mix_norm_w, ap_w_in, ap_sinks, pool_w, pool_scale, ap_w_out, ssd_w_in, ssd_conv_w, ssd_conv_b, ssd_dt_bias, ssd_A_log, ssd_D, ssd_norm_w, ssd_w_out, ffn_norm_w, w_gate, w_up, w_down, final_norm_w):
    b, s, d = x.shape
    depth = mix_norm_w.shape[0]
    heads = ssd_dt_bias.shape[1]
    inner = heads * SSD_HEAD_DIM
    conv_dim = inner + 2 * SSD_GROUPS * SSD_STATE
    even_tile = min(EVEN_TILE, s)
    odd_tile = min(ODD_TILE, s)
    ffn_tile = min(FFN_TILE, b * s)

    ap_w_in_b, ap_w_out_b = _to_bf16(ap_w_in), _to_bf16(ap_w_out)
    ssd_w_in_t = jnp.swapaxes(ssd_w_in, 1, 2)

    def ffn_casts(layer):
        return (_Cast(w_gate, layer), _Cast(w_up, layer), _Cast(w_down, layer))

    def ssd_casts(i):
        return (_Cast(ssd_w_in_t, i, transpose=True), _Cast(ssd_w_out, i))

    def lane_pad(a):
        return jnp.pad(a, ((0, 0), (0, LANES - heads)))

    expand = jnp.pad(jnp.repeat(jnp.eye(heads, dtype=BF16), SSD_HEAD_DIM, axis=1),
                     ((0, LANES - heads), (0, 0)))

    for layer in range(depth):
        i = layer // 2
        nw = _row(mix_norm_w[layer])
        if layer % 2 == 0:
            sink_rep = jnp.broadcast_to(
                ap_sinks[i].astype(F32).reshape(ATTN_KV_HEADS, ATTN_GROUP, 1, 1),
                (ATTN_KV_HEADS, ATTN_GROUP, ATTN_BLOCK, LANES)).reshape(ATTN_KV_HEADS, ATTN_GROUP * ATTN_BLOCK, LANES)
            x, ffn_w = _even_mixer(x, nw, ap_w_in_b, sink_rep, pool_w[i].astype(BF16), _row(pool_scale[i]),
                                   ap_w_out_b, i, tile=even_tile, casts=ffn_casts(layer))
        else:
            x, ffn_w = _odd_mixer(x, nw, ssd_w[0], ssd_conv_w[i].astype(F32), _row(ssd_conv_b[i]),
                                  lane_pad(_row(ssd_dt_bias[i])), _col(ssd_dt_bias[i]),
                                  lane_pad(_row(ssd_A_log[i])), _col(ssd_A_log[i]),
                                  _row(jnp.repeat(ssd_D[i], SSD_HEAD_DIM)), _row(ssd_norm_w[i]), expand,
                                  ssd_w[1], tile=odd_tile, casts=ffn_casts(layer))
        final = layer == depth - 1
        next_is_odd = not final and (layer + 1) % 2 == 1
        x2d, ssd_w = _ffn(x.reshape(b * s, d), _row(ffn_norm_w[layer]), *ffn_w, _row(final_norm_w),
                          final=final, tile=ffn_tile, casts=ssd_casts((layer + 1) // 2) if next_is_odd else ())
        x = x2d.reshape(b, s, d)
    return x
```

```python
import functools
from typing import NamedTuple

import jax
import jax.numpy as jnp
from jax import lax
from jax.experimental import pallas as pl
from jax.experimental.pallas import tpu as pltpu

F32 = jnp.float32
BF16 = jnp.bfloat16

EPS = 1e-5
LANES = 128
BF16_SUBLANES = 16
PIECE_COLS = 256
ATTN_HEADS = 8
ATTN_KV_HEADS = 2
HEAD_DIM = 64
ATTN_BLOCK = 128
ATTN_GROUP = ATTN_HEADS // ATTN_KV_HEADS
ATTN_WIDTH = ATTN_HEADS * HEAD_DIM
KV_WIDTH = ATTN_KV_HEADS * HEAD_DIM
POOL_WINDOWS = (2, 4, 8, 16)
POOL_GROUP_DIM = 128
POOL_WIDTH = len(POOL_WINDOWS) * POOL_GROUP_DIM
POOL_CARRY = 16
SSD_HEAD_DIM = 64
SSD_GROUPS = 4
SSD_STATE = 128
SSD_CONV = 4
SSD_CHUNK = 128
CONV_CARRY = 8
assert SSD_CONV == 4
NEG = -1e30
LOG2E = 1.4426950408889634

VMEM_LIMIT = 56 * 1024 * 1024
CAST_BLOCK_BYTES = 6 * 1024 * 1024

EVEN_TILE = 512
ODD_TILE = 256
FFN_TILE = 512


def _dot(a, b):
    return jnp.dot(a, b, preferred_element_type=F32)


def _dot_nt(a, b):
    return lax.dot_general(a, b, (((1,), (1,)), ((), ())), preferred_element_type=F32)


def _rmsnorm(x, w):
    ms = jnp.mean(x * x, axis=-1, keepdims=True)
    return x * lax.rsqrt(ms + EPS) * w


def _silu(x):
    h = 0.5 * x
    return h + h * jnp.tanh(h)


def _split3(v):
    hi = v.astype(BF16)
    r = v - hi.astype(F32)
    mid = r.astype(BF16)
    lo = (r - mid.astype(F32)).astype(BF16)
    return hi, mid, lo


def _const_spec(shape):
    nd = len(shape)
    return pl.BlockSpec(shape, lambda *_: (0,) * nd, pipeline_mode=pl.Buffered(1))


def _layer_spec(w, layer):
    return pl.BlockSpec((None,) + w.shape[1:], lambda *_: (layer, 0, 0), pipeline_mode=pl.Buffered(1))


def _cast_kernel(x_ref, o_ref, *, cols):
    aligned = cols // LANES * LANES
    o_ref[0, :, :aligned] = x_ref[0, :, :aligned].astype(BF16)
    if aligned < cols:
        o_ref[0, :, aligned:] = jnp.zeros((o_ref.shape[1], o_ref.shape[2] - aligned), BF16)
        o_ref[0, :, aligned:cols] = x_ref[0, :, aligned:].astype(BF16)


def _to_bf16(w):
    nl, r, c = w.shape
    c_out = -(-c // LANES) * LANES
    rb = r
    while rb * c * 4 > CAST_BLOCK_BYTES and rb % 32 == 0:
        rb //= 2
    return pl.pallas_call(
        functools.partial(_cast_kernel, cols=c),
        grid=(nl, r // rb),
        in_specs=[pl.BlockSpec((1, rb, c), lambda i, j: (i, j, 0))],
        out_specs=pl.BlockSpec((1, rb, c_out), lambda i, j: (i, j, 0)),
        out_shape=jax.ShapeDtypeStruct((nl, r, c_out), BF16),
        compiler_params=pltpu.CompilerParams(
            dimension_semantics=("arbitrary", "arbitrary"), vmem_limit_bytes=VMEM_LIMIT),
        name="cast_bf16",
    )(w)


class _Cast(NamedTuple):
    w: jax.Array
    layer: int
    transpose: bool = False


def _cast_plan(cast, nsteps):
    if cast.transpose:
        _, c, r = cast.w.shape
        nstrips = max(k for k in range(1, nsteps + 1) if nsteps % k == 0 and (r // LANES) % k == 0)
        c_out = -(-c // LANES) * LANES
        plan = ((None, c, r // nstrips), lambda s: (cast.layer, 0, s), (r // nstrips, c_out), (r, c_out))
    else:
        _, r, c = cast.w.shape
        nstrips = max(k for k in range(1, nsteps + 1)
                      if nsteps % k == 0 and r % k == 0 and (r // k) % BF16_SUBLANES == 0)
        rb = r // nstrips
        plan = ((None, rb, c), lambda s: (cast.layer, s, 0), (rb, c), (r, c))
    assert nsteps % nstrips == 0
    return plan + (nsteps // nstrips,)


def _emit_cast(transpose, reps, step, i_ref, o_ref):
    if not transpose:
        o_ref[...] = i_ref[...].astype(BF16)
        return

    @pl.when(step % reps == 0)
    def _():
        c, width = i_ref.shape
        aligned = c // LANES * LANES
        xt = i_ref[...]
        parts = [xt[:aligned].T]
        if aligned < c:
            tail = jnp.concatenate([xt[aligned:], jnp.zeros((LANES - (c - aligned), width), F32)], axis=0)
            parts.append(tail.T)
        o_ref[...] = jnp.concatenate(parts, axis=1).astype(BF16)


def _fused_kernel(*refs, body, n_in, cast_meta, grid):
    n_c = len(cast_meta)
    ins, cast_ins = refs[:n_in], refs[n_in:n_in + n_c]
    out, cast_outs = refs[n_in + n_c], refs[n_in + n_c + 1:n_in + 2 * n_c + 1]
    scratch = refs[n_in + 2 * n_c + 1:]
    body(*ins, out, *scratch)
    step = pl.program_id(0)
    for axis in range(1, len(grid)):
        step = step * grid[axis] + pl.program_id(axis)
    for (transpose, reps), i_ref, o_ref in zip(cast_meta, cast_ins, cast_outs):
        _emit_cast(transpose, reps, step, i_ref, o_ref)


def _fused_call(body, *, name, grid, in_specs, operands, out_spec, out_shape, scratch_shapes=(), casts=()):
    nsteps = 1
    for g in grid:
        nsteps *= g

    def flat(idx):
        s = idx[0]
        for axis in range(1, len(grid)):
            s = s * grid[axis] + idx[axis]
        return s

    cast_in_specs, cast_out_specs, cast_out_shapes, cast_meta = [], [], [], []
    for cast in casts:
        in_block, in_index, out_block, o_shape, reps = _cast_plan(cast, nsteps)
        cast_in_specs.append(pl.BlockSpec(in_block, lambda *g, f=in_index, r=reps: f(flat(g) // r)))
        cast_out_specs.append(pl.BlockSpec(out_block, lambda *g, r=reps: (flat(g) // r, 0)))
        cast_out_shapes.append(jax.ShapeDtypeStruct(o_shape, BF16))
        cast_meta.append((cast.transpose, reps))
    outs = pl.pallas_call(
        functools.partial(_fused_kernel, body=body, n_in=len(in_specs), cast_meta=tuple(cast_meta), grid=grid),
        grid=grid,
        in_specs=list(in_specs) + cast_in_specs,
        out_specs=[out_spec] + cast_out_specs,
        out_shape=[out_shape] + cast_out_shapes,
        scratch_shapes=list(scratch_shapes),
        compiler_params=pltpu.CompilerParams(
            dimension_semantics=("arbitrary",) * len(grid), vmem_limit_bytes=VMEM_LIMIT),
        name=name,
    )(*operands, *[c.w for c in casts])
    return outs[0], list(outs[1:])


def _weight_spec(w, layer):
    return _const_spec(w.shape) if w.ndim == 2 else _layer_spec(w, layer)


def _ffn_kernel(x_ref, nw_ref, wg_ref, wu_ref, wd_ref, fw_ref, o_ref, *, final):
    x = x_ref[...]
    h = _rmsnorm(x, nw_ref[...]).astype(BF16)
    g = _dot(h, wg_ref[...])
    u = _dot(h, wu_ref[...])
    a = (_silu(g) * u).astype(BF16)
    y = x + _dot(a, wd_ref[...])
    if final:
        y = _rmsnorm(y, fw_ref[...])
    o_ref[...] = y


def _ffn(x2d, nw, wg, wu, wd, fw, *, final, tile, casts=()):
    n, d = x2d.shape
    return _fused_call(
        functools.partial(_ffn_kernel, final=final),
        name="ffn",
        grid=(n // tile,),
        in_specs=[
            pl.BlockSpec((tile, d), lambda i: (i, 0)),
            _const_spec((1, d)),
            _const_spec(wg.shape),
            _const_spec(wu.shape),
            _const_spec(wd.shape),
            _const_spec((1, d)),
        ],
        operands=(x2d, nw, wg, wu, wd, fw),
        out_spec=pl.BlockSpec((tile, d), lambda i: (i, 0)),
        out_shape=jax.ShapeDtypeStruct((n, d), F32),
        casts=casts,
    )


def _even_kernel(x_ref, nw_ref, win_ref, sink_ref, pw_ref, ps_ref, wout_ref, o_ref,
                 kprev_ref, vprev_ref, pcarry_ref, attn_ref, *, tile):
    t = pl.program_id(1)
    nblk = tile // ATTN_BLOCK
    blk = ATTN_BLOCK

    @pl.when(t == 0)
    def _():
        kprev_ref[...] = jnp.zeros_like(kprev_ref)
        vprev_ref[...] = jnp.zeros_like(vprev_ref)
        pcarry_ref[...] = jnp.zeros_like(pcarry_ref)

    x = x_ref[0]
    h = _rmsnorm(x, nw_ref[...]).astype(BF16)
    proj = _dot(h, win_ref[...])
    q = proj[:, :ATTN_WIDTH] * (HEAD_DIM ** -0.5 * LOG2E)
    k = proj[:, ATTN_WIDTH:ATTN_WIDTH + KV_WIDTH]
    v = proj[:, ATTN_WIDTH + KV_WIDTH:ATTN_WIDTH + 2 * KV_WIDTH]
    p = proj[:, ATTN_WIDTH + 2 * KV_WIDTH:]
    q_is_even = lax.broadcasted_iota(jnp.int32, q.shape, 1) % LANES < HEAD_DIM
    q_par = (jnp.where(q_is_even, q, 0.0).astype(BF16), jnp.where(q_is_even, 0.0, q).astype(BF16))

    lane = lax.broadcasted_iota(jnp.int32, (tile, KV_WIDTH), 1)
    lo_half = lane < HEAD_DIM

    def both_halves(a):
        sw = pltpu.roll(a, HEAD_DIM, 1)
        return (jnp.where(lo_half, a, sw).astype(BF16), jnp.where(lo_half, sw, a).astype(BF16))

    k2 = both_halves(k)
    v2 = both_halves(v)
    kfull = [jnp.concatenate([kprev_ref[hh], k2[hh]], axis=0) for hh in range(ATTN_KV_HEADS)]
    vfull = [jnp.concatenate([vprev_ref[hh], v2[hh]], axis=0) for hh in range(ATTN_KV_HEADS)]
    for hh in range(ATTN_KV_HEADS):
        kprev_ref[hh] = k2[hh][tile - blk:]
        vprev_ref[hh] = v2[hh][tile - blk:]

    rows = ATTN_GROUP * blk
    qi = lax.broadcasted_iota(jnp.int32, (rows, 2 * blk), 0) % blk
    kj = lax.broadcasted_iota(jnp.int32, (rows, 2 * blk), 1)
    band = (kj > qi) & (kj <= qi + blk)
    first_key = jnp.where(t == 0, blk, 0)
    qlane = lax.broadcasted_iota(jnp.int32, (blk, LANES), 1)
    q_lo = qlane < HEAD_DIM
    ones = jnp.ones((2 * blk, LANES), BF16)

    for j in range(nblk):
        valid = band & (kj >= first_key) if j == 0 else band
        for hh in range(ATTN_KV_HEADS):
            parts = []
            for g in range(ATTN_GROUP):
                c0 = (hh * ATTN_GROUP + g) // 2 * LANES
                parts.append(q_par[g % 2][j * blk:(j + 1) * blk, c0:c0 + LANES])
            qst = jnp.concatenate(parts, axis=0)
            kb = kfull[hh][j * blk:(j + 2) * blk]
            vb = jnp.concatenate([vfull[hh][j * blk:(j + 2) * blk], ones], axis=1)
            s = _dot_nt(qst, kb)
            s = jnp.where(valid, s, NEG)
            sink = sink_ref[hh]
            m = jnp.maximum(jnp.max(s, axis=-1, keepdims=True), sink)
            e = jnp.exp2(s - jnp.concatenate([m, m], axis=1))
            r = _dot(e.astype(BF16), vb)
            den = r[:, LANES:] + jnp.exp2(sink - m)
            o = r[:, :LANES] * (1.0 / den)
            for a in range(ATTN_GROUP // 2):
                slab = jnp.where(q_lo, o[(2 * a) * blk:(2 * a + 1) * blk],
                                 o[(2 * a + 1) * blk:(2 * a + 2) * blk])
                c0 = (hh * ATTN_GROUP // 2 + a) * LANES
                attn_ref[j * blk:(j + 1) * blk, c0:c0 + LANES] = slab.astype(BF16)

    pe = jnp.concatenate([pcarry_ref[...], p], axis=0)
    pcarry_ref[...] = p[tile - POOL_CARRY:]
    cnt = (t * tile + 1 + lax.broadcasted_iota(jnp.int32, (tile, POOL_GROUP_DIM), 0)).astype(F32)
    pooled = []
    for gi, w in enumerate(POOL_WINDOWS):
        u = pe[:, gi * POOL_GROUP_DIM:(gi + 1) * POOL_GROUP_DIM]
        sm = u
        sh = 1
        while sh < w:
            sm = sm + pltpu.roll(sm, sh, 0)
            sh *= 2
        mean = sm[POOL_CARRY:] / jnp.minimum(cnt, float(w))
        d = (mean - u[POOL_CARRY:]).astype(BF16)
        yg = _dot(d, pw_ref[gi]) * ps_ref[:, gi * POOL_GROUP_DIM:(gi + 1) * POOL_GROUP_DIM]
        pooled.append(yg.astype(BF16))
    mixed = jnp.concatenate(pooled, axis=1)
    o_ref[0] = (x + _dot(attn_ref[...], wout_ref[:ATTN_WIDTH, :])
                + _dot(mixed, wout_ref[ATTN_WIDTH:, :]))


def _even_mixer(x, nw, w_in, sink_rep, pw, ps, w_out, layer, *, tile, casts=()):
    b, s, d = x.shape
    return _fused_call(
        functools.partial(_even_kernel, tile=tile),
        name="even_mixer",
        grid=(b, s // tile),
        in_specs=[
            pl.BlockSpec((1, tile, d), lambda i, j: (i, j, 0)),
            _const_spec(nw.shape),
            _weight_spec(w_in, layer),
            _const_spec(sink_rep.shape),
            _const_spec(pw.shape),
            _const_spec(ps.shape),
            _weight_spec(w_out, layer),
        ],
        operands=(x, nw, w_in, sink_rep, pw, ps, w_out),
        out_spec=pl.BlockSpec((1, tile, d), lambda i, j: (i, j, 0)),
        out_shape=jax.ShapeDtypeStruct((b, s, d), F32),
        scratch_shapes=[
            pltpu.VMEM((ATTN_KV_HEADS, ATTN_BLOCK, KV_WIDTH), BF16),
            pltpu.VMEM((ATTN_KV_HEADS, ATTN_BLOCK, KV_WIDTH), BF16),
            pltpu.VMEM((POOL_CARRY, POOL_WIDTH), F32),
            pltpu.VMEM((tile, ATTN_WIDTH), BF16),
        ],
        casts=casts,
    )


def _softplus(x):
    return jnp.maximum(x, 0.0) + jnp.log(1.0 + jnp.exp(-jnp.abs(x)))


def _odd_kernel(x_ref, nw_ref, win_ref, cw_ref, cb_ref, dtb_ref, dtbT_ref,
                alog_ref, alogT_ref, dexp_ref, gnw_ref, expand_ref, wout_ref, o_ref,
                state_ref, ccarry_ref, act_ref, z_ref, y_ref, *, tile, inner, heads):
    t = pl.program_id(1)
    L = SSD_CHUNK
    nchunk = tile // L
    gn = SSD_GROUPS * SSD_STATE
    conv_dim = inner + 2 * gn
    heads_per_group = heads // SSD_GROUPS
    pairs_per_group = heads_per_group // 2
    gw = heads_per_group * SSD_HEAD_DIM

    @pl.when(t == 0)
    def _():
        state_ref[...] = jnp.zeros_like(state_ref)
        ccarry_ref[...] = jnp.zeros_like(ccarry_ref)

    x = x_ref[0]
    hn = _rmsnorm(x, nw_ref[...]).astype(BF16)

    def conv_cols(c0, width):
        xb = _dot(hn, win_ref[:, inner + c0:inner + c0 + width])
        xe = jnp.concatenate([ccarry_ref[:, c0:c0 + width], xb], axis=0)
        ccarry_ref[:, c0:c0 + width] = xb[tile - CONV_CARRY:]
        x1 = pltpu.roll(xe, 1, 0)
        w0, w1, w2, w3 = (cw_ref[kk:kk + 1, c0:c0 + width] for kk in range(SSD_CONV))
        acc = (xe * w3 + x1 * w2) + pltpu.roll(xe * w1 + x1 * w0, 2, 0)
        act_ref[:, c0:c0 + width] = _silu(acc[CONV_CARRY:] + cb_ref[:, c0:c0 + width])

    def z_cols(c0, width):
        z_ref[:, c0:c0 + width] = _dot(hn, win_ref[:, c0:c0 + width])

    def project_group_items(g):
        items = []
        for c0 in range(g * gw, (g + 1) * gw, PIECE_COLS):
            items.append(functools.partial(conv_cols, c0, PIECE_COLS))
            items.append(functools.partial(z_cols, c0, PIECE_COLS))
        return items

    def finish_group_items(g):
        g0 = g * gw
        cell = []

        def gate():
            yg = y_ref[:, g0:g0 + gw] + act_ref[:, g0:g0 + gw] * dexp_ref[:, g0:g0 + gw]
            yg = yg * _silu(z_ref[:, g0:g0 + gw])
            ms = jnp.mean(yg * yg, axis=-1, keepdims=True)
            cell.append((yg * lax.rsqrt(ms + EPS) * gnw_ref[:, g0:g0 + gw]).astype(BF16))

        def out_proj():
            part = _dot(cell.pop(), wout_ref[g0:g0 + gw, :])
            if g == 0:
                o_ref[0] = x + part
            else:
                o_ref[0] = o_ref[0] + part

        return [gate, out_proj]

    for c0 in range(inner, conv_dim, PIECE_COLS):
        conv_cols(c0, PIECE_COLS)
    dtr = _dot(hn, win_ref[:, inner + conv_dim:])
    dtrT = dtr.T[:heads]

    a_neg = -jnp.exp(alog_ref[...])
    a_negT = -jnp.exp(alogT_ref[...])

    li = lax.broadcasted_iota(jnp.int32, (L, L), 0)
    si = lax.broadcasted_iota(jnp.int32, (L, L), 1)
    causal = si <= li
    tril = jnp.where(causal, 1.0, 0.0).astype(BF16)
    triu = jnp.where(li <= si, 1.0, 0.0).astype(BF16)
    plane = lax.broadcasted_iota(jnp.int32, (L, LANES), 1)
    p_lo = plane < SSD_HEAD_DIM

    chunks = []
    for c in range(nchunk):
        r0 = c * L
        dt = _softplus(dtr[r0:r0 + L] + dtb_ref[...])
        dtT = _softplus(dtrT[:, r0:r0 + L] + dtbT_ref[...])
        a_cs = sum(_dot(tril, part) for part in _split3(dt * a_neg))
        a_csT = sum(_dot(part, triu) for part in _split3(dtT * a_negT))
        a_last = a_cs[L - 1:L, :]
        a_lastT = a_csT[:, L - 1:L]
        a_last16 = jnp.broadcast_to(a_last, (16, a_last.shape[1]))
        chunks.append(dict(
            a2=a_cs * LOG2E,
            wT=dtT * jnp.exp(a_lastT - a_csT),
            a2_shiftT=a_csT * LOG2E - jnp.log2(dtT),
            cd_row=jnp.exp(sum(_dot(part, expand_ref[...]) for part in _split3(a_last16))[0:1]),
        ))

    for item in project_group_items(0):
        item()
    pending = []
    for g in range(SSD_GROUPS):
        if g + 1 < SSD_GROUPS:
            pending.extend(project_group_items(g + 1))
        g0 = g * gw
        for c, ck in enumerate(chunks):
            r0 = c * L
            bg = act_ref[r0:r0 + L, inner + g * SSD_STATE:inner + (g + 1) * SSD_STATE]
            cg = act_ref[r0:r0 + L, inner + gn + g * SSD_STATE:inner + gn + (g + 1) * SSD_STATE].astype(BF16)
            cb = _dot_nt(cg, bg.astype(BF16))
            bgT = bg.T
            y_off = _dot(cg, state_ref[:, g0:g0 + gw].astype(BF16))
            for pr in range(pairs_per_group):
                h0 = g * heads_per_group + 2 * pr
                c0 = h0 * SSD_HEAD_DIM
                xp = act_ref[r0:r0 + L, c0:c0 + LANES]
                w_a = jnp.concatenate([jnp.where(p_lo, xp, 0.0), jnp.where(p_lo, 0.0, xp)], axis=0).astype(BF16)
                m_parts, b_parts, e_parts = [], [], []
                for hd in (h0, h0 + 1):
                    acol = jnp.broadcast_to(ck["a2"][:, hd:hd + 1], (L, L))
                    dec = jnp.exp2(jnp.where(causal, acol - ck["a2_shiftT"][hd:hd + 1, :], NEG))
                    m_parts.append((cb * dec).astype(BF16))
                    b_parts.append((bgT * ck["wT"][hd:hd + 1, :]).astype(BF16))
                    e_parts.append(jnp.exp2(acol))
                lhs = jnp.concatenate([jnp.concatenate(m_parts, axis=1),
                                       jnp.concatenate(b_parts, axis=1)], axis=0)
                ra = _dot(lhs, w_a)
                ea_pair = jnp.where(p_lo, e_parts[0], e_parts[1])
                y_ref[r0:r0 + L, c0:c0 + LANES] = ra[:L] + y_off[:, c0 - g0:c0 - g0 + LANES] * ea_pair
                state_ref[:, c0:c0 + LANES] = (state_ref[:, c0:c0 + LANES] * ck["cd_row"][:, c0:c0 + LANES]
                                               + ra[L:])
                if pending:
                    pending.pop(0)()
        while pending:
            pending.pop(0)()
        pending.extend(finish_group_items(g))
    while pending:
        pending.pop(0)()


def _odd_mixer(x, nw, w_in, cw, cb, dtb, dtbT, alog, alogT, dexp, gnw, expand, w_out, *, tile, casts=()):
    b, s, d = x.shape
    heads = dtbT.shape[0]
    inner = heads * SSD_HEAD_DIM
    conv_dim = inner + 2 * SSD_GROUPS * SSD_STATE
    operands = (x, nw, w_in, cw, cb, dtb, dtbT, alog, alogT, dexp, gnw, expand, w_out)
    return _fused_call(
        functools.partial(_odd_kernel, tile=tile, inner=inner, heads=heads),
        name="odd_mixer",
        grid=(b, s // tile),
        in_specs=[pl.BlockSpec((1, tile, d), lambda i, j: (i, j, 0))]
        + [_const_spec(c.shape) for c in operands[1:]],
        operands=operands,
        out_spec=pl.BlockSpec((1, tile, d), lambda i, j: (i, j, 0)),
        out_shape=jax.ShapeDtypeStruct((b, s, d), F32),
        scratch_shapes=[
            pltpu.VMEM((SSD_STATE, inner), F32),
            pltpu.VMEM((CONV_CARRY, conv_dim), F32),
            pltpu.VMEM((tile, conv_dim), F32),
            pltpu.VMEM((tile, inner), F32),
            pltpu.VMEM((tile, inner), F32),
        ],
        casts=casts,
    )


def _row(v):
    return v.reshape(1, -1).astype(F32)


def _col(v):
    return v.reshape(-1, 1).astype(F32)


def kernel(x, mix_norm_w, ap_w_in, ap_sinks, pool_w, pool_scale, ap_w_out, ssd_w_in, ssd_conv_w, ssd_conv_b, ssd_dt_bias, ssd_A_log, ssd_D, ssd_norm_w, ssd_w_out, ffn_norm_w, w_gate, w_up, w_down, final_norm_w):
    b, s, d = x.shape
    depth = mix_norm_w.shape[0]
    heads = ssd_dt_bias.shape[1]
    inner = heads * SSD_HEAD_DIM
    conv_dim = inner + 2 * SSD_GROUPS * SSD_STATE
    even_tile = min(EVEN_TILE, s)
    odd_tile = min(ODD_TILE, s)
    ffn_tile = min(FFN_TILE, b * s)

    ap_w_in_b, ap_w_out_b = _to_bf16(ap_w_in), _to_bf16(ap_w_out)
    ssd_w_in_t = jnp.swapaxes(ssd_w_in, 1, 2)

    def ffn_casts(layer):
        return (_Cast(w_gate, layer), _Cast(w_up, layer), _Cast(w_down, layer))

    def ssd_casts(i):
        return (_Cast(ssd_w_in_t, i, transpose=True), _Cast(ssd_w_out, i))

    def lane_pad(a):
        return jnp.pad(a, ((0, 0), (0, LANES - heads)))

    expand = jnp.pad(jnp.repeat(jnp.eye(heads, dtype=BF16), SSD_HEAD_DIM, axis=1),
                     ((0, LANES - heads), (0, 0)))

    for layer in range(depth):
        i = layer // 2
        nw = _row(mix_norm_w[layer])
        if layer % 2 == 0:
            sink_rep = jnp.broadcast_to(
                (ap_sinks[i].astype(F32) * LOG2E).reshape(ATTN_KV_HEADS, ATTN_GROUP, 1, 1),
                (ATTN_KV_HEADS, ATTN_GROUP, ATTN_BLOCK, LANES)).reshape(ATTN_KV_HEADS, ATTN_GROUP * ATTN_BLOCK, LANES)
            x, ffn_w = _even_mixer(x, nw, ap_w_in_b, sink_rep, pool_w[i].astype(BF16), _row(pool_scale[i]),
                                   ap_w_out_b, i, tile=even_tile, casts=ffn_casts(layer))
        else:
            x, ffn_w = _odd_mixer(x, nw, ssd_w[0], ssd_conv_w[i].astype(F32), _row(ssd_conv_b[i]),
                                  lane_pad(_row(ssd_dt_bias[i])), _col(ssd_dt_bias[i]),
                                  lane_pad(_row(ssd_A_log[i])), _col(ssd_A_log[i]),
                                  _row(jnp.repeat(ssd_D[i], SSD_HEAD_DIM)), _row(ssd_norm_w[i]), expand,
                                  ssd_w[1], tile=odd_tile, casts=ffn_casts(layer))
        final = layer == depth - 1
        next_is_odd = not final and (layer + 1) % 2 == 1
        x2d, ssd_w = _ffn(x.reshape(b * s, d), _row(ffn_norm_w[layer]), *ffn_w, _row(final_norm_w),
                          final=final, tile=ffn_tile, casts=ssd_casts((layer + 1) // 2) if next_is_odd else ())
        x = x2d.reshape(b, s, d)
    return x
```

```python
import functools
from typing import NamedTuple

import jax
import jax.numpy as jnp
from jax import lax
from jax.experimental import pallas as pl
from jax.experimental.pallas import tpu as pltpu

F32 = jnp.float32
BF16 = jnp.bfloat16

EPS = 1e-5
LANES = 128
BF16_SUBLANES = 16
PIECE_COLS = 256
ATTN_HEADS = 8
ATTN_KV_HEADS = 2
HEAD_DIM = 64
ATTN_BLOCK = 128
ATTN_GROUP = ATTN_HEADS // ATTN_KV_HEADS
ATTN_WIDTH = ATTN_HEADS * HEAD_DIM
KV_WIDTH = ATTN_KV_HEADS * HEAD_DIM
POOL_WINDOWS = (2, 4, 8, 16)
POOL_GROUP_DIM = 128
POOL_WIDTH = len(POOL_WINDOWS) * POOL_GROUP_DIM
POOL_CARRY = 16
SSD_HEAD_DIM = 64
SSD_GROUPS = 4
SSD_STATE = 128
SSD_CONV = 4
SSD_CHUNK = 128
CONV_CARRY = 8
assert SSD_CONV == 4
NEG = -1e30
LOG2E = 1.4426950408889634

VMEM_LIMIT = 56 * 1024 * 1024
CAST_BLOCK_BYTES = 6 * 1024 * 1024

EVEN_TILE = 512
ODD_TILE = 256
FFN_TILE = 512


def _dot(a, b):
    return jnp.dot(a, b, preferred_element_type=F32)


def _dot_nt(a, b):
    return lax.dot_general(a, b, (((1,), (1,)), ((), ())), preferred_element_type=F32)


def _rmsnorm(x, w):
    ms = jnp.mean(x * x, axis=-1, keepdims=True)
    return x * lax.rsqrt(ms + EPS) * w


def _silu(x):
    h = 0.5 * x
    return h + h * jnp.tanh(h)


def _split3(v):
    hi = v.astype(BF16)
    r = v - hi.astype(F32)
    mid = r.astype(BF16)
    lo = (r - mid.astype(F32)).astype(BF16)
    return hi, mid, lo


def _const_spec(shape):
    nd = len(shape)
    return pl.BlockSpec(shape, lambda *_: (0,) * nd, pipeline_mode=pl.Buffered(1))


def _layer_spec(w, layer):
    return pl.BlockSpec((None,) + w.shape[1:], lambda *_: (layer, 0, 0), pipeline_mode=pl.Buffered(1))


def _cast_kernel(x_ref, o_ref, *, cols):
    aligned = cols // LANES * LANES
    o_ref[0, :, :aligned] = x_ref[0, :, :aligned].astype(BF16)
    if aligned < cols:
        o_ref[0, :, aligned:] = jnp.zeros((o_ref.shape[1], o_ref.shape[2] - aligned), BF16)
        o_ref[0, :, aligned:cols] = x_ref[0, :, aligned:].astype(BF16)


def _to_bf16(w):
    nl, r, c = w.shape
    c_out = -(-c // LANES) * LANES
    rb = r
    while rb * c * 4 > CAST_BLOCK_BYTES and rb % 32 == 0:
        rb //= 2
    return pl.pallas_call(
        functools.partial(_cast_kernel, cols=c),
        grid=(nl, r // rb),
        in_specs=[pl.BlockSpec((1, rb, c), lambda i, j: (i, j, 0))],
        out_specs=pl.BlockSpec((1, rb, c_out), lambda i, j: (i, j, 0)),
        out_shape=jax.ShapeDtypeStruct((nl, r, c_out), BF16),
        compiler_params=pltpu.CompilerParams(
            dimension_semantics=("arbitrary", "arbitrary"), vmem_limit_bytes=VMEM_LIMIT),
        name="cast_bf16",
    )(w)


class _Cast(NamedTuple):
    w: jax.Array
    layer: int
    transpose: bool = False


def _cast_plan(cast, nsteps):
    if cast.transpose:
        _, c, r = cast.w.shape
        nstrips = max(k for k in range(1, nsteps + 1) if nsteps % k == 0 and (r // LANES) % k == 0)
        c_out = -(-c // LANES) * LANES
        plan = ((None, c, r // nstrips), lambda s: (cast.layer, 0, s), (r // nstrips, c_out), (r, c_out))
    else:
        _, r, c = cast.w.shape
        nstrips = max(k for k in range(1, nsteps + 1)
                      if nsteps % k == 0 and r % k == 0 and (r // k) % BF16_SUBLANES == 0)
        rb = r // nstrips
        plan = ((None, rb, c), lambda s: (cast.layer, s, 0), (rb, c), (r, c))
    assert nsteps % nstrips == 0
    return plan + (nsteps // nstrips,)


def _emit_cast(transpose, reps, step, i_ref, o_ref):
    if not transpose:
        o_ref[...] = i_ref[...].astype(BF16)
        return

    @pl.when(step % reps == 0)
    def _():
        c, width = i_ref.shape
        aligned = c // LANES * LANES
        xt = i_ref[...]
        parts = [xt[:aligned].T]
        if aligned < c:
            tail = jnp.concatenate([xt[aligned:], jnp.zeros((LANES - (c - aligned), width), F32)], axis=0)
            parts.append(tail.T)
        o_ref[...] = jnp.concatenate(parts, axis=1).astype(BF16)


def _fused_kernel(*refs, body, n_in, cast_meta, grid):
    n_c = len(cast_meta)
    ins, cast_ins = refs[:n_in], refs[n_in:n_in + n_c]
    out, cast_outs = refs[n_in + n_c], refs[n_in + n_c + 1:n_in + 2 * n_c + 1]
    scratch = refs[n_in + 2 * n_c + 1:]
    body(*ins, out, *scratch)
    step = pl.program_id(0)
    for axis in range(1, len(grid)):
        step = step * grid[axis] + pl.program_id(axis)
    for (transpose, reps), i_ref, o_ref in zip(cast_meta, cast_ins, cast_outs):
        _emit_cast(transpose, reps, step, i_ref, o_ref)


def _fused_call(body, *, name, grid, in_specs, operands, out_spec, out_shape, scratch_shapes=(), casts=()):
    nsteps = 1
    for g in grid:
        nsteps *= g

    def flat(idx):
        s = idx[0]
        for axis in range(1, len(grid)):
            s = s * grid[axis] + idx[axis]
        return s

    cast_in_specs, cast_out_specs, cast_out_shapes, cast_meta = [], [], [], []
    for cast in casts:
        in_block, in_index, out_block, o_shape, reps = _cast_plan(cast, nsteps)
        cast_in_specs.append(pl.BlockSpec(in_block, lambda *g, f=in_index, r=reps: f(flat(g) // r)))
        cast_out_specs.append(pl.BlockSpec(out_block, lambda *g, r=reps: (flat(g) // r, 0)))
        cast_out_shapes.append(jax.ShapeDtypeStruct(o_shape, BF16))
        cast_meta.append((cast.transpose, reps))
    outs = pl.pallas_call(
        functools.partial(_fused_kernel, body=body, n_in=len(in_specs), cast_meta=tuple(cast_meta), grid=grid),
        grid=grid,
        in_specs=list(in_specs) + cast_in_specs,
        out_specs=[out_spec] + cast_out_specs,
        out_shape=[out_shape] + cast_out_shapes,
        scratch_shapes=list(scratch_shapes),
        compiler_params=pltpu.CompilerParams(
            dimension_semantics=("arbitrary",) * len(grid), vmem_limit_bytes=VMEM_LIMIT),
        name=name,
    )(*operands, *[c.w for c in casts])
    return outs[0], list(outs[1:])


def _weight_spec(w, layer):
    return _const_spec(w.shape) if w.ndim == 2 else _layer_spec(w, layer)


def _ffn_kernel(x_ref, nw_ref, wg_ref, wu_ref, wd_ref, fw_ref, o_ref, *, final):
    x = x_ref[...]
    h = _rmsnorm(x, nw_ref[...]).astype(BF16)
    g = _dot(h, wg_ref[...])
    u = _dot(h, wu_ref[...])
    a = (_silu(g) * u).astype(BF16)
    y = x + _dot(a, wd_ref[...])
    if final:
        y = _rmsnorm(y, fw_ref[...])
    o_ref[...] = y


def _ffn(x2d, nw, wg, wu, wd, fw, *, final, tile, casts=()):
    n, d = x2d.shape
    return _fused_call(
        functools.partial(_ffn_kernel, final=final),
        name="ffn",
        grid=(n // tile,),
        in_specs=[
            pl.BlockSpec((tile, d), lambda i: (i, 0)),
            _const_spec((1, d)),
            _const_spec(wg.shape),
            _const_spec(wu.shape),
            _const_spec(wd.shape),
            _const_spec((1, d)),
        ],
        operands=(x2d, nw, wg, wu, wd, fw),
        out_spec=pl.BlockSpec((tile, d), lambda i: (i, 0)),
        out_shape=jax.ShapeDtypeStruct((n, d), F32),
        casts=casts,
    )


def _even_kernel(x_ref, nw_ref, win_ref, sink_ref, pw_ref, ps_ref, wout_ref, o_ref,
                 kprev_ref, vprev_ref, pcarry_ref, attn_ref, *, tile):
    t = pl.program_id(1)
    nblk = tile // ATTN_BLOCK
    blk = ATTN_BLOCK

    @pl.when(t == 0)
    def _():
        kprev_ref[...] = jnp.zeros_like(kprev_ref)
        vprev_ref[...] = jnp.zeros_like(vprev_ref)
        pcarry_ref[...] = jnp.zeros_like(pcarry_ref)

    x = x_ref[0]
    h = _rmsnorm(x, nw_ref[...]).astype(BF16)
    proj = _dot(h, win_ref[...])
    q = proj[:, :ATTN_WIDTH] * (HEAD_DIM ** -0.5 * LOG2E)
    k = proj[:, ATTN_WIDTH:ATTN_WIDTH + KV_WIDTH]
    v = proj[:, ATTN_WIDTH + KV_WIDTH:ATTN_WIDTH + 2 * KV_WIDTH]
    p = proj[:, ATTN_WIDTH + 2 * KV_WIDTH:]
    q_is_even = lax.broadcasted_iota(jnp.int32, q.shape, 1) % LANES < HEAD_DIM
    q_par = (jnp.where(q_is_even, q, 0.0).astype(BF16), jnp.where(q_is_even, 0.0, q).astype(BF16))

    lane = lax.broadcasted_iota(jnp.int32, (tile, KV_WIDTH), 1)
    lo_half = lane < HEAD_DIM

    def both_halves(a):
        sw = pltpu.roll(a, HEAD_DIM, 1)
        return (jnp.where(lo_half, a, sw).astype(BF16), jnp.where(lo_half, sw, a).astype(BF16))

    k2 = both_halves(k)
    v2 = both_halves(v)
    kfull = [jnp.concatenate([kprev_ref[hh], k2[hh]], axis=0) for hh in range(ATTN_KV_HEADS)]
    vfull = [jnp.concatenate([vprev_ref[hh], v2[hh]], axis=0) for hh in range(ATTN_KV_HEADS)]
    for hh in range(ATTN_KV_HEADS):
        kprev_ref[hh] = k2[hh][tile - blk:]
        vprev_ref[hh] = v2[hh][tile - blk:]

    rows = ATTN_GROUP * blk
    qi = lax.broadcasted_iota(jnp.int32, (rows, 2 * blk), 0) % blk
    kj = lax.broadcasted_iota(jnp.int32, (rows, 2 * blk), 1)
    band = (kj > qi) & (kj <= qi + blk)
    first_key = jnp.where(t == 0, blk, 0)
    qlane = lax.broadcasted_iota(jnp.int32, (blk, LANES), 1)
    q_lo = qlane < HEAD_DIM
    ones = jnp.ones((2 * blk, LANES), BF16)

    for j in range(nblk):
        valid = band & (kj >= first_key) if j == 0 else band
        for hh in range(ATTN_KV_HEADS):
            parts = []
            for g in range(ATTN_GROUP):
                c0 = (hh * ATTN_GROUP + g) // 2 * LANES
                parts.append(q_par[g % 2][j * blk:(j + 1) * blk, c0:c0 + LANES])
            qst = jnp.concatenate(parts, axis=0)
            kb = kfull[hh][j * blk:(j + 2) * blk]
            vb = jnp.concatenate([vfull[hh][j * blk:(j + 2) * blk], ones], axis=1)
            s = _dot_nt(qst, kb)
            s = jnp.where(valid, s, NEG)
            sink = sink_ref[hh]
            m = jnp.maximum(jnp.max(s, axis=-1, keepdims=True), sink)
            e = jnp.exp2(s - jnp.concatenate([m, m], axis=1))
            r = _dot(e.astype(BF16), vb)
            den = r[:, LANES:] + jnp.exp2(sink - m)
            o = r[:, :LANES] * (1.0 / den)
            for a in range(ATTN_GROUP // 2):
                slab = jnp.where(q_lo, o[(2 * a) * blk:(2 * a + 1) * blk],
                                 o[(2 * a + 1) * blk:(2 * a + 2) * blk])
                c0 = (hh * ATTN_GROUP // 2 + a) * LANES
                attn_ref[j * blk:(j + 1) * blk, c0:c0 + LANES] = slab.astype(BF16)

    pe = jnp.concatenate([pcarry_ref[...], p], axis=0)
    pcarry_ref[...] = p[tile - POOL_CARRY:]
    cnt = (t * tile + 1 + lax.broadcasted_iota(jnp.int32, (tile, POOL_GROUP_DIM), 0)).astype(F32)
    pooled = []
    for gi, w in enumerate(POOL_WINDOWS):
        u = pe[:, gi * POOL_GROUP_DIM:(gi + 1) * POOL_GROUP_DIM]
        sm = u
        sh = 1
        while sh < w:
            sm = sm + pltpu.roll(sm, sh, 0)
            sh *= 2
        mean = sm[POOL_CARRY:] / jnp.minimum(cnt, float(w))
        d = (mean - u[POOL_CARRY:]).astype(BF16)
        yg = _dot(d, pw_ref[gi]) * ps_ref[:, gi * POOL_GROUP_DIM:(gi + 1) * POOL_GROUP_DIM]
        pooled.append(yg.astype(BF16))
    mixed = jnp.concatenate(pooled, axis=1)
    o_ref[0] = (x + _dot(attn_ref[...], wout_ref[:ATTN_WIDTH, :])
                + _dot(mixed, wout_ref[ATTN_WIDTH:, :]))


def _even_mixer(x, nw, w_in, sink_rep, pw, ps, w_out, layer, *, tile, casts=()):
    b, s, d = x.shape
    return _fused_call(
        functools.partial(_even_kernel, tile=tile),
        name="even_mixer",
        grid=(b, s // tile),
        in_specs=[
            pl.BlockSpec((1, tile, d), lambda i, j: (i, j, 0)),
            _const_spec(nw.shape),
            _weight_spec(w_in, layer),
            _const_spec(sink_rep.shape),
            _const_spec(pw.shape),
            _const_spec(ps.shape),
            _weight_spec(w_out, layer),
        ],
        operands=(x, nw, w_in, sink_rep, pw, ps, w_out),
        out_spec=pl.BlockSpec((1, tile, d), lambda i, j: (i, j, 0)),
        out_shape=jax.ShapeDtypeStruct((b, s, d), F32),
        scratch_shapes=[
            pltpu.VMEM((ATTN_KV_HEADS, ATTN_BLOCK, KV_WIDTH), BF16),
            pltpu.VMEM((ATTN_KV_HEADS, ATTN_BLOCK, KV_WIDTH), BF16),
            pltpu.VMEM((POOL_CARRY, POOL_WIDTH), F32),
            pltpu.VMEM((tile, ATTN_WIDTH), BF16),
        ],
        casts=casts,
    )


def _softplus(x):
    return jnp.maximum(x, 0.0) + jnp.log(1.0 + jnp.exp(-jnp.abs(x)))


def _odd_kernel(x_ref, nw_ref, win_ref, cw_ref, cb_ref, dtb_ref, alog_ref, dexp_ref, gnw_ref, wout_ref, o_ref,
                state_ref, ccarry_ref, act_ref, z_ref, y_ref, *, tile, inner, heads):
    t = pl.program_id(1)
    L = SSD_CHUNK
    nchunk = tile // L
    gn = SSD_GROUPS * SSD_STATE
    conv_dim = inner + 2 * gn
    heads_per_group = heads // SSD_GROUPS
    pairs_per_group = heads_per_group // 2
    gw = heads_per_group * SSD_HEAD_DIM

    @pl.when(t == 0)
    def _():
        state_ref[...] = jnp.zeros_like(state_ref)
        ccarry_ref[...] = jnp.zeros_like(ccarry_ref)

    x = x_ref[0]
    hn = _rmsnorm(x, nw_ref[...]).astype(BF16)

    def conv_cols(c0, width):
        xb = _dot(hn, win_ref[:, inner + c0:inner + c0 + width])
        xe = jnp.concatenate([ccarry_ref[:, c0:c0 + width], xb], axis=0)
        ccarry_ref[:, c0:c0 + width] = xb[tile - CONV_CARRY:]
        x1 = pltpu.roll(xe, 1, 0)
        w0, w1, w2, w3 = (cw_ref[kk:kk + 1, c0:c0 + width] for kk in range(SSD_CONV))
        acc = (xe * w3 + x1 * w2) + pltpu.roll(xe * w1 + x1 * w0, 2, 0)
        act_ref[:, c0:c0 + width] = _silu(acc[CONV_CARRY:] + cb_ref[:, c0:c0 + width])

    def z_cols(c0, width):
        z_ref[:, c0:c0 + width] = _dot(hn, win_ref[:, c0:c0 + width])

    def project_group_items(g):
        items = []
        for c0 in range(g * gw, (g + 1) * gw, PIECE_COLS):
            items.append(functools.partial(conv_cols, c0, PIECE_COLS))
            items.append(functools.partial(z_cols, c0, PIECE_COLS))
        return items

    def finish_group_items(g):
        g0 = g * gw
        cell = []

        def gate():
            yg = y_ref[:, g0:g0 + gw] + act_ref[:, g0:g0 + gw] * dexp_ref[:, g0:g0 + gw]
            yg = yg * _silu(z_ref[:, g0:g0 + gw])
            ms = jnp.mean(yg * yg, axis=-1, keepdims=True)
            cell.append((yg * lax.rsqrt(ms + EPS) * gnw_ref[:, g0:g0 + gw]).astype(BF16))

        def out_proj():
            part = _dot(cell.pop(), wout_ref[g0:g0 + gw, :])
            if g == 0:
                o_ref[0] = x + part
            else:
                o_ref[0] = o_ref[0] + part

        return [gate, out_proj]

    fill = [functools.partial(conv_cols, c0, PIECE_COLS) for c0 in range(inner, conv_dim, PIECE_COLS)]
    fill += project_group_items(0)

    def emit_fill(n):
        for _ in range(n):
            if fill:
                fill.pop(0)()

    dtr = _dot(hn, win_ref[:, inner + conv_dim:])
    emit_fill(1)

    a_neg = -jnp.exp(alog_ref[...])

    li = lax.broadcasted_iota(jnp.int32, (L, L), 0)
    si = lax.broadcasted_iota(jnp.int32, (L, L), 1)
    causal = si <= li
    tril = jnp.where(causal, 1.0, 0.0).astype(BF16)
    plane = lax.broadcasted_iota(jnp.int32, (L, LANES), 1)
    p_lo = plane < SSD_HEAD_DIM

    chunks = []
    for c in range(nchunk):
        r0 = c * L
        dt = _softplus(dtr[r0:r0 + L] + dtb_ref[...])
        a_cs = sum(_dot(tril, part) for part in _split3(dt * a_neg))
        emit_fill(1)
        a2 = a_cs * LOG2E
        w = dt * jnp.exp(a_cs[L - 1:L, :] - a_cs)
        shift = a2 - jnp.log2(dt)
        chunks.append(dict(a2=a2, wT=w.T[:heads], a2_shiftT=shift.T[:heads]))
        emit_fill(1)
    emit_fill(len(fill))

    pending = []
    for g in range(SSD_GROUPS):
        if g + 1 < SSD_GROUPS:
            pending.extend(project_group_items(g + 1))
        g0 = g * gw
        for c, ck in enumerate(chunks):
            r0 = c * L
            bg = act_ref[r0:r0 + L, inner + g * SSD_STATE:inner + (g + 1) * SSD_STATE]
            cg = act_ref[r0:r0 + L, inner + gn + g * SSD_STATE:inner + gn + (g + 1) * SSD_STATE].astype(BF16)
            cb = _dot_nt(cg, bg.astype(BF16)).astype(BF16)
            bgT = bg.T.astype(BF16)
            y_off = _dot(cg, state_ref[:, g0:g0 + gw].astype(BF16))
            for pr in range(pairs_per_group):
                h0 = g * heads_per_group + 2 * pr
                c0 = h0 * SSD_HEAD_DIM
                xp = act_ref[r0:r0 + L, c0:c0 + LANES]
                w_a = jnp.concatenate([jnp.where(p_lo, xp, 0.0), jnp.where(p_lo, 0.0, xp)], axis=0).astype(BF16)
                m_parts, b_parts, e_parts = [], [], []
                for hd in (h0, h0 + 1):
                    acol = jnp.broadcast_to(ck["a2"][:, hd:hd + 1], (L, L))
                    dec = jnp.exp2(jnp.where(causal, acol - ck["a2_shiftT"][hd:hd + 1, :], NEG))
                    m_parts.append(cb * dec.astype(BF16))
                    b_parts.append(bgT * ck["wT"][hd:hd + 1, :].astype(BF16))
                    e_parts.append(jnp.exp2(acol))
                lhs = jnp.concatenate([jnp.concatenate(m_parts, axis=1),
                                       jnp.concatenate(b_parts, axis=1)], axis=0)
                ra = _dot(lhs, w_a)
                ea_pair = jnp.where(p_lo, e_parts[0], e_parts[1])
                y_ref[r0:r0 + L, c0:c0 + LANES] = ra[:L] + y_off[:, c0 - g0:c0 - g0 + LANES] * ea_pair
                state_ref[:, c0:c0 + LANES] = state_ref[:, c0:c0 + LANES] * ea_pair[L - 1:L, :] + ra[L:]
                if pending:
                    pending.pop(0)()
        while pending:
            pending.pop(0)()
        pending.extend(finish_group_items(g))
    while pending:
        pending.pop(0)()


def _odd_mixer(x, nw, w_in, cw, cb, dtb, alog, dexp, gnw, w_out, *, heads, tile, casts=()):
    b, s, d = x.shape
    inner = heads * SSD_HEAD_DIM
    conv_dim = inner + 2 * SSD_GROUPS * SSD_STATE
    operands = (x, nw, w_in, cw, cb, dtb, alog, dexp, gnw, w_out)
    return _fused_call(
        functools.partial(_odd_kernel, tile=tile, inner=inner, heads=heads),
        name="odd_mixer",
        grid=(b, s // tile),
        in_specs=[pl.BlockSpec((1, tile, d), lambda i, j: (i, j, 0))]
        + [_const_spec(c.shape) for c in operands[1:]],
        operands=operands,
        out_spec=pl.BlockSpec((1, tile, d), lambda i, j: (i, j, 0)),
        out_shape=jax.ShapeDtypeStruct((b, s, d), F32),
        scratch_shapes=[
            pltpu.VMEM((SSD_STATE, inner), F32),
            pltpu.VMEM((CONV_CARRY, conv_dim), F32),
            pltpu.VMEM((tile, conv_dim), F32),
            pltpu.VMEM((tile, inner), F32),
            pltpu.VMEM((tile, inner), F32),
        ],
        casts=casts,
    )


def _row(v):
    return v.reshape(1, -1).astype(F32)


def kernel(x, mix_norm_w, ap_w_in, ap_sinks, pool_w, pool_scale, ap_w_out, ssd_w_in, ssd_conv_w, ssd_conv_b, ssd_dt_bias, ssd_A_log, ssd_D, ssd_norm_w, ssd_w_out, ffn_norm_w, w_gate, w_up, w_down, final_norm_w):
    b, s, d = x.shape
    depth = mix_norm_w.shape[0]
    heads = ssd_dt_bias.shape[1]
    inner = heads * SSD_HEAD_DIM
    conv_dim = inner + 2 * SSD_GROUPS * SSD_STATE
    even_tile = min(EVEN_TILE, s)
    odd_tile = min(ODD_TILE, s)
    ffn_tile = min(FFN_TILE, b * s)

    ap_w_in_b, ap_w_out_b = _to_bf16(ap_w_in), _to_bf16(ap_w_out)
    ssd_w_in_t = jnp.swapaxes(ssd_w_in, 1, 2)

    def ffn_casts(layer):
        return (_Cast(w_gate, layer), _Cast(w_up, layer), _Cast(w_down, layer))

    def ssd_casts(i):
        return (_Cast(ssd_w_in_t, i, transpose=True), _Cast(ssd_w_out, i))

    def lane_pad(a):
        return jnp.pad(a, ((0, 0), (0, LANES - heads)))

    for layer in range(depth):
        i = layer // 2
        nw = _row(mix_norm_w[layer])
        if layer % 2 == 0:
            sink_rep = jnp.broadcast_to(
                (ap_sinks[i].astype(F32) * LOG2E).reshape(ATTN_KV_HEADS, ATTN_GROUP, 1, 1),
                (ATTN_KV_HEADS, ATTN_GROUP, ATTN_BLOCK, LANES)).reshape(ATTN_KV_HEADS, ATTN_GROUP * ATTN_BLOCK, LANES)
            x, ffn_w = _even_mixer(x, nw, ap_w_in_b, sink_rep, pool_w[i].astype(BF16), _row(pool_scale[i]),
                                   ap_w_out_b, i, tile=even_tile, casts=ffn_casts(layer))
        else:
            x, ffn_w = _odd_mixer(x, nw, ssd_w[0], ssd_conv_w[i].astype(F32), _row(ssd_conv_b[i]),
                                  lane_pad(_row(ssd_dt_bias[i])), lane_pad(_row(ssd_A_log[i])),
                                  _row(jnp.repeat(ssd_D[i], SSD_HEAD_DIM)), _row(ssd_norm_w[i]),
                                  ssd_w[1], heads=heads, tile=odd_tile, casts=ffn_casts(layer))
        final = layer == depth - 1
        next_is_odd = not final and (layer + 1) % 2 == 1
        x2d, ssd_w = _ffn(x.reshape(b * s, d), _row(ffn_norm_w[layer]), *ffn_w, _row(final_norm_w),
                          final=final, tile=ffn_tile, casts=ssd_casts((layer + 1) // 2) if next_is_odd else ())
        x = x2d.reshape(b, s, d)
    return x
```

```python
import functools
from typing import NamedTuple

import jax
import jax.numpy as jnp
from jax import lax
from jax.experimental import pallas as pl
from jax.experimental.pallas import tpu as pltpu

F32 = jnp.float32
BF16 = jnp.bfloat16

EPS = 1e-5
LANES = 128
BF16_SUBLANES = 16
PIECE_COLS = 256
ATTN_HEADS = 8
ATTN_KV_HEADS = 2
HEAD_DIM = 64
ATTN_BLOCK = 128
ATTN_GROUP = ATTN_HEADS // ATTN_KV_HEADS
ATTN_WIDTH = ATTN_HEADS * HEAD_DIM
KV_WIDTH = ATTN_KV_HEADS * HEAD_DIM
POOL_WINDOWS = (2, 4, 8, 16)
POOL_GROUP_DIM = 128
POOL_WIDTH = len(POOL_WINDOWS) * POOL_GROUP_DIM
POOL_CARRY = 16
SSD_HEAD_DIM = 64
SSD_GROUPS = 4
SSD_STATE = 128
SSD_CONV = 4
SSD_CHUNK = 128
CONV_CARRY = 8
assert SSD_CONV == 4
NEG = -1e30
LOG2E = 1.4426950408889634

VMEM_LIMIT = 56 * 1024 * 1024
CAST_BLOCK_BYTES = 6 * 1024 * 1024

EVEN_TILE = 512
ODD_TILE = 512
FFN_TILE = 512


def _dot(a, b):
    return jnp.dot(a, b, preferred_element_type=F32)


def _dot_nt(a, b):
    return lax.dot_general(a, b, (((1,), (1,)), ((), ())), preferred_element_type=F32)


def _rmsnorm(x, w):
    ms = jnp.mean(x * x, axis=-1, keepdims=True)
    return x * lax.rsqrt(ms + EPS) * w


def _silu(x):
    h = 0.5 * x
    return h + h * jnp.tanh(h)


def _split3(v):
    hi = v.astype(BF16)
    r = v - hi.astype(F32)
    mid = r.astype(BF16)
    lo = (r - mid.astype(F32)).astype(BF16)
    return hi, mid, lo


def _const_spec(shape):
    nd = len(shape)
    return pl.BlockSpec(shape, lambda *_: (0,) * nd, pipeline_mode=pl.Buffered(1))


def _layer_spec(w, layer):
    return pl.BlockSpec((None,) + w.shape[1:], lambda *_: (layer, 0, 0), pipeline_mode=pl.Buffered(1))


def _cast_kernel(x_ref, o_ref, *, cols):
    aligned = cols // LANES * LANES
    o_ref[0, :, :aligned] = x_ref[0, :, :aligned].astype(BF16)
    if aligned < cols:
        o_ref[0, :, aligned:] = jnp.zeros((o_ref.shape[1], o_ref.shape[2] - aligned), BF16)
        o_ref[0, :, aligned:cols] = x_ref[0, :, aligned:].astype(BF16)


def _to_bf16(w):
    nl, r, c = w.shape
    c_out = -(-c // LANES) * LANES
    rb = r
    while rb * c * 4 > CAST_BLOCK_BYTES and rb % 32 == 0:
        rb //= 2
    return pl.pallas_call(
        functools.partial(_cast_kernel, cols=c),
        grid=(nl, r // rb),
        in_specs=[pl.BlockSpec((1, rb, c), lambda i, j: (i, j, 0))],
        out_specs=pl.BlockSpec((1, rb, c_out), lambda i, j: (i, j, 0)),
        out_shape=jax.ShapeDtypeStruct((nl, r, c_out), BF16),
        compiler_params=pltpu.CompilerParams(
            dimension_semantics=("arbitrary", "arbitrary"), vmem_limit_bytes=VMEM_LIMIT),
        name="cast_bf16",
    )(w)


class _Cast(NamedTuple):
    w: jax.Array
    layer: int
    transpose: bool = False


def _cast_plan(cast, nsteps):
    if cast.transpose:
        _, c, r = cast.w.shape
        nstrips = max(k for k in range(1, nsteps + 1) if nsteps % k == 0 and (r // LANES) % k == 0)
        c_out = -(-c // LANES) * LANES
        plan = ((None, c, r // nstrips), lambda s: (cast.layer, 0, s), (r // nstrips, c_out), (r, c_out))
    else:
        _, r, c = cast.w.shape
        nstrips = max(k for k in range(1, nsteps + 1)
                      if nsteps % k == 0 and r % k == 0 and (r // k) % BF16_SUBLANES == 0)
        rb = r // nstrips
        plan = ((None, rb, c), lambda s: (cast.layer, s, 0), (rb, c), (r, c))
    assert nsteps % nstrips == 0
    return plan + (nsteps // nstrips,)


def _emit_cast(transpose, reps, step, i_ref, o_ref):
    if not transpose:
        o_ref[...] = i_ref[...].astype(BF16)
        return

    @pl.when(step % reps == 0)
    def _():
        c, width = i_ref.shape
        aligned = c // LANES * LANES
        xt = i_ref[...]
        parts = [xt[:aligned].T]
        if aligned < c:
            tail = jnp.concatenate([xt[aligned:], jnp.zeros((LANES - (c - aligned), width), F32)], axis=0)
            parts.append(tail.T)
        o_ref[...] = jnp.concatenate(parts, axis=1).astype(BF16)


def _fused_kernel(*refs, body, n_in, cast_meta, grid):
    n_c = len(cast_meta)
    ins, cast_ins = refs[:n_in], refs[n_in:n_in + n_c]
    out, cast_outs = refs[n_in + n_c], refs[n_in + n_c + 1:n_in + 2 * n_c + 1]
    scratch = refs[n_in + 2 * n_c + 1:]
    body(*ins, out, *scratch)
    step = pl.program_id(0)
    for axis in range(1, len(grid)):
        step = step * grid[axis] + pl.program_id(axis)
    for (transpose, reps), i_ref, o_ref in zip(cast_meta, cast_ins, cast_outs):
        _emit_cast(transpose, reps, step, i_ref, o_ref)


def _fused_call(body, *, name, grid, in_specs, operands, out_spec, out_shape, scratch_shapes=(), casts=()):
    nsteps = 1
    for g in grid:
        nsteps *= g

    def flat(idx):
        s = idx[0]
        for axis in range(1, len(grid)):
            s = s * grid[axis] + idx[axis]
        return s

    cast_in_specs, cast_out_specs, cast_out_shapes, cast_meta = [], [], [], []
    for cast in casts:
        in_block, in_index, out_block, o_shape, reps = _cast_plan(cast, nsteps)
        cast_in_specs.append(pl.BlockSpec(in_block, lambda *g, f=in_index, r=reps: f(flat(g) // r)))
        cast_out_specs.append(pl.BlockSpec(out_block, lambda *g, r=reps: (flat(g) // r, 0)))
        cast_out_shapes.append(jax.ShapeDtypeStruct(o_shape, BF16))
        cast_meta.append((cast.transpose, reps))
    outs = pl.pallas_call(
        functools.partial(_fused_kernel, body=body, n_in=len(in_specs), cast_meta=tuple(cast_meta), grid=grid),
        grid=grid,
        in_specs=list(in_specs) + cast_in_specs,
        out_specs=[out_spec] + cast_out_specs,
        out_shape=[out_shape] + cast_out_shapes,
        scratch_shapes=list(scratch_shapes),
        compiler_params=pltpu.CompilerParams(
            dimension_semantics=("arbitrary",) * len(grid), vmem_limit_bytes=VMEM_LIMIT),
        name=name,
    )(*operands, *[c.w for c in casts])
    return outs[0], list(outs[1:])


def _weight_spec(w, layer):
    return _const_spec(w.shape) if w.ndim == 2 else _layer_spec(w, layer)


def _ffn_kernel(x_ref, nw_ref, wg_ref, wu_ref, wd_ref, fw_ref, o_ref, *, final):
    x = x_ref[...]
    h = _rmsnorm(x, nw_ref[...]).astype(BF16)
    g = _dot(h, wg_ref[...])
    u = _dot(h, wu_ref[...])
    a = (_silu(g) * u).astype(BF16)
    y = x + _dot(a, wd_ref[...])
    if final:
        y = _rmsnorm(y, fw_ref[...])
    o_ref[...] = y


def _ffn(x2d, nw, wg, wu, wd, fw, *, final, tile, casts=()):
    n, d = x2d.shape
    return _fused_call(
        functools.partial(_ffn_kernel, final=final),
        name="ffn",
        grid=(n // tile,),
        in_specs=[
            pl.BlockSpec((tile, d), lambda i: (i, 0)),
            _const_spec((1, d)),
            _const_spec(wg.shape),
            _const_spec(wu.shape),
            _const_spec(wd.shape),
            _const_spec((1, d)),
        ],
        operands=(x2d, nw, wg, wu, wd, fw),
        out_spec=pl.BlockSpec((tile, d), lambda i: (i, 0)),
        out_shape=jax.ShapeDtypeStruct((n, d), F32),
        casts=casts,
    )


def _even_kernel(x_ref, nw_ref, win_ref, sink_ref, pw_ref, ps_ref, wout_ref, o_ref,
                 kprev_ref, vprev_ref, pcarry_ref, attn_ref, *, tile):
    t = pl.program_id(1)
    nblk = tile // ATTN_BLOCK
    blk = ATTN_BLOCK

    @pl.when(t == 0)
    def _():
        kprev_ref[...] = jnp.zeros_like(kprev_ref)
        vprev_ref[...] = jnp.zeros_like(vprev_ref)
        pcarry_ref[...] = jnp.zeros_like(pcarry_ref)

    x = x_ref[0]
    h = _rmsnorm(x, nw_ref[...]).astype(BF16)
    proj = _dot(h, win_ref[...])
    q = proj[:, :ATTN_WIDTH] * (HEAD_DIM ** -0.5 * LOG2E)
    k = proj[:, ATTN_WIDTH:ATTN_WIDTH + KV_WIDTH]
    v = proj[:, ATTN_WIDTH + KV_WIDTH:ATTN_WIDTH + 2 * KV_WIDTH]
    p = proj[:, ATTN_WIDTH + 2 * KV_WIDTH:]
    q_is_even = lax.broadcasted_iota(jnp.int32, q.shape, 1) % LANES < HEAD_DIM
    q_par = (jnp.where(q_is_even, q, 0.0).astype(BF16), jnp.where(q_is_even, 0.0, q).astype(BF16))

    lane = lax.broadcasted_iota(jnp.int32, (tile, KV_WIDTH), 1)
    lo_half = lane < HEAD_DIM

    def both_halves(a):
        sw = pltpu.roll(a, HEAD_DIM, 1)
        return (jnp.where(lo_half, a, sw).astype(BF16), jnp.where(lo_half, sw, a).astype(BF16))

    k2 = both_halves(k)
    v2 = both_halves(v)
    kfull = [jnp.concatenate([kprev_ref[hh], k2[hh]], axis=0) for hh in range(ATTN_KV_HEADS)]
    vfull = [jnp.concatenate([vprev_ref[hh], v2[hh]], axis=0) for hh in range(ATTN_KV_HEADS)]
    for hh in range(ATTN_KV_HEADS):
        kprev_ref[hh] = k2[hh][tile - blk:]
        vprev_ref[hh] = v2[hh][tile - blk:]

    rows = ATTN_GROUP * blk
    qi = lax.broadcasted_iota(jnp.int32, (rows, 2 * blk), 0) % blk
    kj = lax.broadcasted_iota(jnp.int32, (rows, 2 * blk), 1)
    band = (kj > qi) & (kj <= qi + blk)
    first_key = jnp.where(t == 0, blk, 0)
    qlane = lax.broadcasted_iota(jnp.int32, (blk, LANES), 1)
    q_lo = qlane < HEAD_DIM
    ones = jnp.ones((2 * blk, LANES), BF16)

    for j in range(nblk):
        valid = band & (kj >= first_key) if j == 0 else band
        for hh in range(ATTN_KV_HEADS):
            parts = []
            for g in range(ATTN_GROUP):
                c0 = (hh * ATTN_GROUP + g) // 2 * LANES
                parts.append(q_par[g % 2][j * blk:(j + 1) * blk, c0:c0 + LANES])
            qst = jnp.concatenate(parts, axis=0)
            kb = kfull[hh][j * blk:(j + 2) * blk]
            vb = jnp.concatenate([vfull[hh][j * blk:(j + 2) * blk], ones], axis=1)
            s = _dot_nt(qst, kb)
            s = jnp.where(valid, s, NEG)
            sink = sink_ref[hh]
            m = jnp.maximum(jnp.max(s, axis=-1, keepdims=True), sink)
            e = jnp.exp2(s - jnp.concatenate([m, m], axis=1))
            r = _dot(e.astype(BF16), vb)
            den = r[:, LANES:] + jnp.exp2(sink - m)
            o = r[:, :LANES] * (1.0 / den)
            for a in range(ATTN_GROUP // 2):
                slab = jnp.where(q_lo, o[(2 * a) * blk:(2 * a + 1) * blk],
                                 o[(2 * a + 1) * blk:(2 * a + 2) * blk])
                c0 = (hh * ATTN_GROUP // 2 + a) * LANES
                attn_ref[j * blk:(j + 1) * blk, c0:c0 + LANES] = slab.astype(BF16)

    pe = jnp.concatenate([pcarry_ref[...], p], axis=0)
    pcarry_ref[...] = p[tile - POOL_CARRY:]
    cnt = (t * tile + 1 + lax.broadcasted_iota(jnp.int32, (tile, POOL_GROUP_DIM), 0)).astype(F32)
    pooled = []
    for gi, w in enumerate(POOL_WINDOWS):
        u = pe[:, gi * POOL_GROUP_DIM:(gi + 1) * POOL_GROUP_DIM]
        sm = u
        sh = 1
        while sh < w:
            sm = sm + pltpu.roll(sm, sh, 0)
            sh *= 2
        mean = sm[POOL_CARRY:] / jnp.minimum(cnt, float(w))
        d = (mean - u[POOL_CARRY:]).astype(BF16)
        yg = _dot(d, pw_ref[gi]) * ps_ref[:, gi * POOL_GROUP_DIM:(gi + 1) * POOL_GROUP_DIM]
        pooled.append(yg.astype(BF16))
    mixed = jnp.concatenate(pooled, axis=1)
    o_ref[0] = (x + _dot(attn_ref[...], wout_ref[:ATTN_WIDTH, :])
                + _dot(mixed, wout_ref[ATTN_WIDTH:, :]))


def _even_mixer(x, nw, w_in, sink_rep, pw, ps, w_out, layer, *, tile, casts=()):
    b, s, d = x.shape
    return _fused_call(
        functools.partial(_even_kernel, tile=tile),
        name="even_mixer",
        grid=(b, s // tile),
        in_specs=[
            pl.BlockSpec((1, tile, d), lambda i, j: (i, j, 0)),
            _const_spec(nw.shape),
            _weight_spec(w_in, layer),
            _const_spec(sink_rep.shape),
            _const_spec(pw.shape),
            _const_spec(ps.shape),
            _weight_spec(w_out, layer),
        ],
        operands=(x, nw, w_in, sink_rep, pw, ps, w_out),
        out_spec=pl.BlockSpec((1, tile, d), lambda i, j: (i, j, 0)),
        out_shape=jax.ShapeDtypeStruct((b, s, d), F32),
        scratch_shapes=[
            pltpu.VMEM((ATTN_KV_HEADS, ATTN_BLOCK, KV_WIDTH), BF16),
            pltpu.VMEM((ATTN_KV_HEADS, ATTN_BLOCK, KV_WIDTH), BF16),
            pltpu.VMEM((POOL_CARRY, POOL_WIDTH), F32),
            pltpu.VMEM((tile, ATTN_WIDTH), BF16),
        ],
        casts=casts,
    )


def _softplus(x):
    return jnp.maximum(x, 0.0) + jnp.log(1.0 + jnp.exp(-jnp.abs(x)))


def _odd_kernel(x_ref, nw_ref, win_ref, cw_ref, cb_ref, dtb_ref, alog_ref, dexp_ref, gnw_ref, wout_ref, o_ref,
                state_ref, ccarry_ref, act_ref, z_ref, y_ref, *, tile, inner, heads):
    t = pl.program_id(1)
    L = SSD_CHUNK
    nchunk = tile // L
    gn = SSD_GROUPS * SSD_STATE
    conv_dim = inner + 2 * gn
    heads_per_group = heads // SSD_GROUPS
    pairs_per_group = heads_per_group // 2
    gw = heads_per_group * SSD_HEAD_DIM

    @pl.when(t == 0)
    def _():
        state_ref[...] = jnp.zeros_like(state_ref)
        ccarry_ref[...] = jnp.zeros_like(ccarry_ref)

    x = x_ref[0]
    hn = _rmsnorm(x, nw_ref[...]).astype(BF16)

    def conv_cols(c0, width):
        xb = _dot(hn, win_ref[:, inner + c0:inner + c0 + width])
        xe = jnp.concatenate([ccarry_ref[:, c0:c0 + width], xb], axis=0)
        ccarry_ref[:, c0:c0 + width] = xb[tile - CONV_CARRY:]
        x1 = pltpu.roll(xe, 1, 0)
        w0, w1, w2, w3 = (cw_ref[kk:kk + 1, c0:c0 + width] for kk in range(SSD_CONV))
        acc = (xe * w3 + x1 * w2) + pltpu.roll(xe * w1 + x1 * w0, 2, 0)
        act_ref[:, c0:c0 + width] = _silu(acc[CONV_CARRY:] + cb_ref[:, c0:c0 + width])

    def z_cols(c0, width):
        z_ref[:, c0:c0 + width] = _dot(hn, win_ref[:, c0:c0 + width])

    def project_group_items(g):
        items = []
        for c0 in range(g * gw, (g + 1) * gw, PIECE_COLS):
            items.append(functools.partial(conv_cols, c0, PIECE_COLS))
            items.append(functools.partial(z_cols, c0, PIECE_COLS))
        return items

    def finish_group_items(g):
        g0 = g * gw
        cell = []

        def gate():
            yg = y_ref[:, g0:g0 + gw] + act_ref[:, g0:g0 + gw] * dexp_ref[:, g0:g0 + gw]
            yg = yg * _silu(z_ref[:, g0:g0 + gw])
            ms = jnp.mean(yg * yg, axis=-1, keepdims=True)
            cell.append((yg * lax.rsqrt(ms + EPS) * gnw_ref[:, g0:g0 + gw]).astype(BF16))

        def out_proj():
            part = _dot(cell.pop(), wout_ref[g0:g0 + gw, :])
            if g == 0:
                o_ref[0] = x + part
            else:
                o_ref[0] = o_ref[0] + part

        return [gate, out_proj]

    fill = [functools.partial(conv_cols, c0, PIECE_COLS) for c0 in range(inner, conv_dim, PIECE_COLS)]
    fill += project_group_items(0)

    def emit_fill(n):
        for _ in range(n):
            if fill:
                fill.pop(0)()

    dtr = _dot(hn, win_ref[:, inner + conv_dim:])
    emit_fill(1)

    a_neg = -jnp.exp(alog_ref[...])

    li = lax.broadcasted_iota(jnp.int32, (L, L), 0)
    si = lax.broadcasted_iota(jnp.int32, (L, L), 1)
    causal = si <= li
    tril = jnp.where(causal, 1.0, 0.0).astype(BF16)
    plane = lax.broadcasted_iota(jnp.int32, (L, LANES), 1)
    p_lo = plane < SSD_HEAD_DIM

    chunks = []
    for c in range(nchunk):
        r0 = c * L
        dt = _softplus(dtr[r0:r0 + L] + dtb_ref[...])
        a_cs = sum(_dot(tril, part) for part in _split3(dt * a_neg))
        emit_fill(1)
        a2 = a_cs * LOG2E
        w = dt * jnp.exp(a_cs[L - 1:L, :] - a_cs)
        shift = a2 - jnp.log2(dt)
        chunks.append(dict(a2=a2, wT=w.T[:heads], a2_shiftT=shift.T[:heads]))
        emit_fill(1)
    emit_fill(len(fill))

    pending = []
    for g in range(SSD_GROUPS):
        if g + 1 < SSD_GROUPS:
            pending.extend(project_group_items(g + 1))
        g0 = g * gw
        for c, ck in enumerate(chunks):
            r0 = c * L
            bg = act_ref[r0:r0 + L, inner + g * SSD_STATE:inner + (g + 1) * SSD_STATE]
            cg = act_ref[r0:r0 + L, inner + gn + g * SSD_STATE:inner + gn + (g + 1) * SSD_STATE].astype(BF16)
            bgT = bg.T.astype(BF16)
            cb = _dot(cg, bgT).astype(BF16)
            y_off = _dot(cg, state_ref[:, g0:g0 + gw].astype(BF16))
            for pr in range(pairs_per_group):
                h0 = g * heads_per_group + 2 * pr
                c0 = h0 * SSD_HEAD_DIM
                xp = act_ref[r0:r0 + L, c0:c0 + LANES]
                w_a = jnp.concatenate([jnp.where(p_lo, xp, 0.0), jnp.where(p_lo, 0.0, xp)], axis=0).astype(BF16)
                m_parts, b_parts, e_parts = [], [], []
                for hd in (h0, h0 + 1):
                    acol = jnp.broadcast_to(ck["a2"][:, hd:hd + 1], (L, L))
                    dec = jnp.exp2(jnp.where(causal, acol - ck["a2_shiftT"][hd:hd + 1, :], NEG))
                    m_parts.append(cb * dec.astype(BF16))
                    b_parts.append(bgT * ck["wT"][hd:hd + 1, :].astype(BF16))
                    e_parts.append(jnp.exp2(acol))
                lhs = jnp.concatenate([jnp.concatenate(m_parts, axis=1),
                                       jnp.concatenate(b_parts, axis=1)], axis=0)
                ra = _dot(lhs, w_a)
                ea_pair = jnp.where(p_lo, e_parts[0], e_parts[1])
                y_ref[r0:r0 + L, c0:c0 + LANES] = ra[:L] + y_off[:, c0 - g0:c0 - g0 + LANES] * ea_pair
                state_ref[:, c0:c0 + LANES] = state_ref[:, c0:c0 + LANES] * ea_pair[L - 1:L, :] + ra[L:]
                if pending:
                    pending.pop(0)()
        while pending:
            pending.pop(0)()
        pending.extend(finish_group_items(g))
    while pending:
        pending.pop(0)()


def _odd_mixer(x, nw, w_in, cw, cb, dtb, alog, dexp, gnw, w_out, *, heads, tile, casts=()):
    b, s, d = x.shape
    inner = heads * SSD_HEAD_DIM
    conv_dim = inner + 2 * SSD_GROUPS * SSD_STATE
    operands = (x, nw, w_in, cw, cb, dtb, alog, dexp, gnw, w_out)
    return _fused_call(
        functools.partial(_odd_kernel, tile=tile, inner=inner, heads=heads),
        name="odd_mixer",
        grid=(b, s // tile),
        in_specs=[pl.BlockSpec((1, tile, d), lambda i, j: (i, j, 0))]
        + [_const_spec(c.shape) for c in operands[1:]],
        operands=operands,
        out_spec=pl.BlockSpec((1, tile, d), lambda i, j: (i, j, 0)),
        out_shape=jax.ShapeDtypeStruct((b, s, d), F32),
        scratch_shapes=[
            pltpu.VMEM((SSD_STATE, inner), F32),
            pltpu.VMEM((CONV_CARRY, conv_dim), F32),
            pltpu.VMEM((tile, conv_dim), F32),
            pltpu.VMEM((tile, inner), F32),
            pltpu.VMEM((tile, inner), F32),
        ],
        casts=casts,
    )


def _row(v):
    return v.reshape(1, -1).astype(F32)


def kernel(x, mix_norm_w, ap_w_in, ap_sinks, pool_w, pool_scale, ap_w_out, ssd_w_in, ssd_conv_w, ssd_conv_b, ssd_dt_bias, ssd_A_log, ssd_D, ssd_norm_w, ssd_w_out, ffn_norm_w, w_gate, w_up, w_down, final_norm_w):
    b, s, d = x.shape
    depth = mix_norm_w.shape[0]
    heads = ssd_dt_bias.shape[1]
    inner = heads * SSD_HEAD_DIM
    conv_dim = inner + 2 * SSD_GROUPS * SSD_STATE
    even_tile = min(EVEN_TILE, s)
    odd_tile = min(ODD_TILE, s)
    ffn_tile = min(FFN_TILE, b * s)

    ap_w_in_b, ap_w_out_b = _to_bf16(ap_w_in), _to_bf16(ap_w_out)
    ssd_w_in_t = jnp.swapaxes(ssd_w_in, 1, 2)

    def ffn_casts(layer):
        return (_Cast(w_gate, layer), _Cast(w_up, layer), _Cast(w_down, layer))

    def ssd_casts(i):
        return (_Cast(ssd_w_in_t, i, transpose=True), _Cast(ssd_w_out, i))

    def lane_pad(a):
        return jnp.pad(a, ((0, 0), (0, LANES - heads)))

    for layer in range(depth):
        i = layer // 2
        nw = _row(mix_norm_w[layer])
        if layer % 2 == 0:
            sink_rep = jnp.broadcast_to(
                (ap_sinks[i].astype(F32) * LOG2E).reshape(ATTN_KV_HEADS, ATTN_GROUP, 1, 1),
                (ATTN_KV_HEADS, ATTN_GROUP, ATTN_BLOCK, LANES)).reshape(ATTN_KV_HEADS, ATTN_GROUP * ATTN_BLOCK, LANES)
            x, ffn_w = _even_mixer(x, nw, ap_w_in_b, sink_rep, pool_w[i].astype(BF16), _row(pool_scale[i]),
                                   ap_w_out_b, i, tile=even_tile, casts=ffn_casts(layer))
        else:
            x, ffn_w = _odd_mixer(x, nw, ssd_w[0], ssd_conv_w[i].astype(F32), _row(ssd_conv_b[i]),
                                  lane_pad(_row(ssd_dt_bias[i])), lane_pad(_row(ssd_A_log[i])),
                                  _row(jnp.repeat(ssd_D[i], SSD_HEAD_DIM)), _row(ssd_norm_w[i]),
                                  ssd_w[1], heads=heads, tile=odd_tile, casts=ffn_casts(layer))
        final = layer == depth - 1
        next_is_odd = not final and (layer + 1) % 2 == 1
        x2d, ssd_w = _ffn(x.reshape(b * s, d), _row(ffn_norm_w[layer]), *ffn_w, _row(final_norm_w),
                          final=final, tile=ffn_tile, casts=ssd_casts((layer + 1) // 2) if next_is_odd else ())
        x = x2d.reshape(b, s, d)
    return x
```

```python
import functools
from typing import NamedTuple

import jax
import jax.numpy as jnp
from jax import lax
from jax.experimental import pallas as pl
from jax.experimental.pallas import tpu as pltpu

F32 = jnp.float32
BF16 = jnp.bfloat16

EPS = 1e-5
LANES = 128
BF16_SUBLANES = 16
PIECE_COLS = 256
ATTN_HEADS = 8
ATTN_KV_HEADS = 2
HEAD_DIM = 64
ATTN_BLOCK = 128
ATTN_GROUP = ATTN_HEADS // ATTN_KV_HEADS
ATTN_WIDTH = ATTN_HEADS * HEAD_DIM
KV_WIDTH = ATTN_KV_HEADS * HEAD_DIM
POOL_WINDOWS = (2, 4, 8, 16)
POOL_GROUP_DIM = 128
POOL_WIDTH = len(POOL_WINDOWS) * POOL_GROUP_DIM
POOL_CARRY = 16
SSD_HEAD_DIM = 64
SSD_GROUPS = 4
SSD_STATE = 128
SSD_CONV = 4
SSD_CHUNK = 128
CONV_CARRY = 8
assert SSD_CONV == 4
NEG = -1e30
LOG2E = 1.4426950408889634

VMEM_LIMIT = 56 * 1024 * 1024
CAST_BLOCK_BYTES = 6 * 1024 * 1024

EVEN_TILE = 1024
ODD_TILE = 512
FFN_TILE = 1024
FFN_HIDDEN_CHUNK = 256


def _dot(a, b):
    return jnp.dot(a, b, preferred_element_type=F32)


def _dot_nt(a, b):
    return lax.dot_general(a, b, (((1,), (1,)), ((), ())), preferred_element_type=F32)


def _rmsnorm(x, w):
    ms = jnp.mean(x * x, axis=-1, keepdims=True)
    return x * lax.rsqrt(ms + EPS) * w


def _silu(x):
    h = 0.5 * x
    return h + h * jnp.tanh(h)


def _split3(v):
    hi = v.astype(BF16)
    r = v - hi.astype(F32)
    mid = r.astype(BF16)
    lo = (r - mid.astype(F32)).astype(BF16)
    return hi, mid, lo


def _const_spec(shape):
    nd = len(shape)
    return pl.BlockSpec(shape, lambda *_: (0,) * nd, pipeline_mode=pl.Buffered(1))


def _layer_spec(w, layer):
    return pl.BlockSpec((None,) + w.shape[1:], lambda *_: (layer, 0, 0), pipeline_mode=pl.Buffered(1))


def _cast_kernel(x_ref, o_ref, *, cols):
    aligned = cols // LANES * LANES
    o_ref[0, :, :aligned] = x_ref[0, :, :aligned].astype(BF16)
    if aligned < cols:
        o_ref[0, :, aligned:] = jnp.zeros((o_ref.shape[1], o_ref.shape[2] - aligned), BF16)
        o_ref[0, :, aligned:cols] = x_ref[0, :, aligned:].astype(BF16)


def _to_bf16(w):
    nl, r, c = w.shape
    c_out = -(-c // LANES) * LANES
    rb = r
    while rb * c * 4 > CAST_BLOCK_BYTES and rb % 32 == 0:
        rb //= 2
    return pl.pallas_call(
        functools.partial(_cast_kernel, cols=c),
        grid=(nl, r // rb),
        in_specs=[pl.BlockSpec((1, rb, c), lambda i, j: (i, j, 0))],
        out_specs=pl.BlockSpec((1, rb, c_out), lambda i, j: (i, j, 0)),
        out_shape=jax.ShapeDtypeStruct((nl, r, c_out), BF16),
        compiler_params=pltpu.CompilerParams(
            dimension_semantics=("arbitrary", "arbitrary"), vmem_limit_bytes=VMEM_LIMIT),
        name="cast_bf16",
    )(w)


class _Cast(NamedTuple):
    w: jax.Array
    layer: int
    transpose: bool = False


def _cast_plan(cast, nsteps):
    if cast.transpose:
        _, c, r = cast.w.shape
        nstrips = max(k for k in range(1, nsteps + 1) if nsteps % k == 0 and (r // LANES) % k == 0)
        c_out = -(-c // LANES) * LANES
        plan = ((None, c, r // nstrips), lambda s: (cast.layer, 0, s), (r // nstrips, c_out), (r, c_out))
    else:
        _, r, c = cast.w.shape
        nstrips = max(k for k in range(1, nsteps + 1)
                      if nsteps % k == 0 and r % k == 0 and (r // k) % BF16_SUBLANES == 0)
        rb = r // nstrips
        plan = ((None, rb, c), lambda s: (cast.layer, s, 0), (rb, c), (r, c))
    assert nsteps % nstrips == 0
    return plan + (nsteps // nstrips,)


def _emit_cast(transpose, reps, step, i_ref, o_ref):
    if not transpose:
        o_ref[...] = i_ref[...].astype(BF16)
        return

    @pl.when(step % reps == 0)
    def _():
        c, width = i_ref.shape
        aligned = c // LANES * LANES
        xt = i_ref[...]
        parts = [xt[:aligned].T]
        if aligned < c:
            tail = jnp.concatenate([xt[aligned:], jnp.zeros((LANES - (c - aligned), width), F32)], axis=0)
            parts.append(tail.T)
        o_ref[...] = jnp.concatenate(parts, axis=1).astype(BF16)


def _fused_kernel(*refs, body, n_in, cast_meta, grid):
    n_c = len(cast_meta)
    ins, cast_ins = refs[:n_in], refs[n_in:n_in + n_c]
    out, cast_outs = refs[n_in + n_c], refs[n_in + n_c + 1:n_in + 2 * n_c + 1]
    scratch = refs[n_in + 2 * n_c + 1:]
    body(*ins, out, *scratch)
    step = pl.program_id(0)
    for axis in range(1, len(grid)):
        step = step * grid[axis] + pl.program_id(axis)
    for (transpose, reps), i_ref, o_ref in zip(cast_meta, cast_ins, cast_outs):
        _emit_cast(transpose, reps, step, i_ref, o_ref)


def _fused_call(body, *, name, grid, in_specs, operands, out_spec, out_shape, scratch_shapes=(), casts=()):
    nsteps = 1
    for g in grid:
        nsteps *= g

    def flat(idx):
        s = idx[0]
        for axis in range(1, len(grid)):
            s = s * grid[axis] + idx[axis]
        return s

    cast_in_specs, cast_out_specs, cast_out_shapes, cast_meta = [], [], [], []
    for cast in casts:
        in_block, in_index, out_block, o_shape, reps = _cast_plan(cast, nsteps)
        cast_in_specs.append(pl.BlockSpec(in_block, lambda *g, f=in_index, r=reps: f(flat(g) // r)))
        cast_out_specs.append(pl.BlockSpec(out_block, lambda *g, r=reps: (flat(g) // r, 0)))
        cast_out_shapes.append(jax.ShapeDtypeStruct(o_shape, BF16))
        cast_meta.append((cast.transpose, reps))
    outs = pl.pallas_call(
        functools.partial(_fused_kernel, body=body, n_in=len(in_specs), cast_meta=tuple(cast_meta), grid=grid),
        grid=grid,
        in_specs=list(in_specs) + cast_in_specs,
        out_specs=[out_spec] + cast_out_specs,
        out_shape=[out_shape] + cast_out_shapes,
        scratch_shapes=list(scratch_shapes),
        compiler_params=pltpu.CompilerParams(
            dimension_semantics=("arbitrary",) * len(grid), vmem_limit_bytes=VMEM_LIMIT),
        name=name,
    )(*operands, *[c.w for c in casts])
    return outs[0], list(outs[1:])


def _weight_spec(w, layer):
    return _const_spec(w.shape) if w.ndim == 2 else _layer_spec(w, layer)


def _ffn_kernel(x_ref, nw_ref, wg_ref, wu_ref, wd_ref, fw_ref, o_ref, *, final):
    x = x_ref[...]
    h = _rmsnorm(x, nw_ref[...]).astype(BF16)
    y = x
    for c0 in range(0, wg_ref.shape[1], FFN_HIDDEN_CHUNK):
        cols = slice(c0, c0 + FFN_HIDDEN_CHUNK)
        g = _dot(h, wg_ref[:, cols])
        u = _dot(h, wu_ref[:, cols])
        a = (_silu(g) * u).astype(BF16)
        y = y + _dot(a, wd_ref[cols, :])
    if final:
        y = _rmsnorm(y, fw_ref[...])
    o_ref[...] = y


def _ffn(x2d, nw, wg, wu, wd, fw, *, final, tile, casts=()):
    n, d = x2d.shape
    return _fused_call(
        functools.partial(_ffn_kernel, final=final),
        name="ffn",
        grid=(n // tile,),
        in_specs=[
            pl.BlockSpec((tile, d), lambda i: (i, 0)),
            _const_spec((1, d)),
            _const_spec(wg.shape),
            _const_spec(wu.shape),
            _const_spec(wd.shape),
            _const_spec((1, d)),
        ],
        operands=(x2d, nw, wg, wu, wd, fw),
        out_spec=pl.BlockSpec((tile, d), lambda i: (i, 0)),
        out_shape=jax.ShapeDtypeStruct((n, d), F32),
        casts=casts,
    )


def _even_kernel(x_ref, nw_ref, win_ref, sink_ref, pw_ref, ps_ref, wout_ref, o_ref,
                 kprev_ref, vprev_ref, pcarry_ref, attn_ref, *, tile):
    t = pl.program_id(1)
    nblk = tile // ATTN_BLOCK
    blk = ATTN_BLOCK

    @pl.when(t == 0)
    def _():
        kprev_ref[...] = jnp.zeros_like(kprev_ref)
        vprev_ref[...] = jnp.zeros_like(vprev_ref)
        pcarry_ref[...] = jnp.zeros_like(pcarry_ref)

    x = x_ref[0]
    h = _rmsnorm(x, nw_ref[...]).astype(BF16)
    proj = _dot(h, win_ref[...])
    q = proj[:, :ATTN_WIDTH] * (HEAD_DIM ** -0.5 * LOG2E)
    k = proj[:, ATTN_WIDTH:ATTN_WIDTH + KV_WIDTH]
    v = proj[:, ATTN_WIDTH + KV_WIDTH:ATTN_WIDTH + 2 * KV_WIDTH]
    p = proj[:, ATTN_WIDTH + 2 * KV_WIDTH:]
    q_is_even = lax.broadcasted_iota(jnp.int32, q.shape, 1) % LANES < HEAD_DIM
    q_par = (jnp.where(q_is_even, q, 0.0).astype(BF16), jnp.where(q_is_even, 0.0, q).astype(BF16))

    lane = lax.broadcasted_iota(jnp.int32, (tile, KV_WIDTH), 1)
    lo_half = lane < HEAD_DIM

    def both_halves(a):
        sw = pltpu.roll(a, HEAD_DIM, 1)
        return (jnp.where(lo_half, a, sw).astype(BF16), jnp.where(lo_half, sw, a).astype(BF16))

    k2 = both_halves(k)
    v2 = both_halves(v)
    kfull = [jnp.concatenate([kprev_ref[hh], k2[hh]], axis=0) for hh in range(ATTN_KV_HEADS)]
    vfull = [jnp.concatenate([vprev_ref[hh], v2[hh]], axis=0) for hh in range(ATTN_KV_HEADS)]
    for hh in range(ATTN_KV_HEADS):
        kprev_ref[hh] = k2[hh][tile - blk:]
        vprev_ref[hh] = v2[hh][tile - blk:]

    rows = ATTN_GROUP * blk
    qi = lax.broadcasted_iota(jnp.int32, (rows, 2 * blk), 0) % blk
    kj = lax.broadcasted_iota(jnp.int32, (rows, 2 * blk), 1)
    band = (kj > qi) & (kj <= qi + blk)
    first_key = jnp.where(t == 0, blk, 0)
    qlane = lax.broadcasted_iota(jnp.int32, (blk, LANES), 1)
    q_lo = qlane < HEAD_DIM
    ones = jnp.ones((2 * blk, LANES), BF16)

    for j in range(nblk):
        valid = band & (kj >= first_key) if j == 0 else band
        for hh in range(ATTN_KV_HEADS):
            parts = []
            for g in range(ATTN_GROUP):
                c0 = (hh * ATTN_GROUP + g) // 2 * LANES
                parts.append(q_par[g % 2][j * blk:(j + 1) * blk, c0:c0 + LANES])
            qst = jnp.concatenate(parts, axis=0)
            kb = kfull[hh][j * blk:(j + 2) * blk]
            vb = jnp.concatenate([vfull[hh][j * blk:(j + 2) * blk], ones], axis=1)
            s = _dot_nt(qst, kb)
            s = jnp.where(valid, s, NEG)
            sink = sink_ref[hh]
            m = jnp.maximum(jnp.max(s, axis=-1, keepdims=True), sink)
            e = jnp.exp2(s - jnp.concatenate([m, m], axis=1))
            r = _dot(e.astype(BF16), vb)
            den = r[:, LANES:] + jnp.exp2(sink - m)
            o = r[:, :LANES] * (1.0 / den)
            for a in range(ATTN_GROUP // 2):
                slab = jnp.where(q_lo, o[(2 * a) * blk:(2 * a + 1) * blk],
                                 o[(2 * a + 1) * blk:(2 * a + 2) * blk])
                c0 = (hh * ATTN_GROUP // 2 + a) * LANES
                attn_ref[j * blk:(j + 1) * blk, c0:c0 + LANES] = slab.astype(BF16)

    pe = jnp.concatenate([pcarry_ref[...], p], axis=0)
    pcarry_ref[...] = p[tile - POOL_CARRY:]
    cnt = (t * tile + 1 + lax.broadcasted_iota(jnp.int32, (tile, POOL_GROUP_DIM), 0)).astype(F32)
    pooled = []
    for gi, w in enumerate(POOL_WINDOWS):
        u = pe[:, gi * POOL_GROUP_DIM:(gi + 1) * POOL_GROUP_DIM]
        sm = u
        sh = 1
        while sh < w:
            sm = sm + pltpu.roll(sm, sh, 0)
            sh *= 2
        mean = sm[POOL_CARRY:] / jnp.minimum(cnt, float(w))
        d = (mean - u[POOL_CARRY:]).astype(BF16)
        yg = _dot(d, pw_ref[gi]) * ps_ref[:, gi * POOL_GROUP_DIM:(gi + 1) * POOL_GROUP_DIM]
        pooled.append(yg.astype(BF16))
    mixed = jnp.concatenate(pooled, axis=1)
    o_ref[0] = (x + _dot(attn_ref[...], wout_ref[:ATTN_WIDTH, :])
                + _dot(mixed, wout_ref[ATTN_WIDTH:, :]))


def _even_mixer(x, nw, w_in, sink_rep, pw, ps, w_out, layer, *, tile, casts=()):
    b, s, d = x.shape
    return _fused_call(
        functools.partial(_even_kernel, tile=tile),
        name="even_mixer",
        grid=(b, s // tile),
        in_specs=[
            pl.BlockSpec((1, tile, d), lambda i, j: (i, j, 0)),
            _const_spec(nw.shape),
            _weight_spec(w_in, layer),
            _const_spec(sink_rep.shape),
            _const_spec(pw.shape),
            _const_spec(ps.shape),
            _weight_spec(w_out, layer),
        ],
        operands=(x, nw, w_in, sink_rep, pw, ps, w_out),
        out_spec=pl.BlockSpec((1, tile, d), lambda i, j: (i, j, 0)),
        out_shape=jax.ShapeDtypeStruct((b, s, d), F32),
        scratch_shapes=[
            pltpu.VMEM((ATTN_KV_HEADS, ATTN_BLOCK, KV_WIDTH), BF16),
            pltpu.VMEM((ATTN_KV_HEADS, ATTN_BLOCK, KV_WIDTH), BF16),
            pltpu.VMEM((POOL_CARRY, POOL_WIDTH), F32),
            pltpu.VMEM((tile, ATTN_WIDTH), BF16),
        ],
        casts=casts,
    )


def _softplus(x):
    return jnp.maximum(x, 0.0) + jnp.log(1.0 + jnp.exp(-jnp.abs(x)))


def _odd_kernel(x_ref, nw_ref, win_ref, cw_ref, cb_ref, dtb_ref, alog_ref, dexp_ref, gnw_ref, wout_ref, o_ref,
                state_ref, ccarry_ref, act_ref, z_ref, y_ref, *, tile, inner, heads):
    t = pl.program_id(1)
    L = SSD_CHUNK
    nchunk = tile // L
    gn = SSD_GROUPS * SSD_STATE
    conv_dim = inner + 2 * gn
    heads_per_group = heads // SSD_GROUPS
    pairs_per_group = heads_per_group // 2
    gw = heads_per_group * SSD_HEAD_DIM

    @pl.when(t == 0)
    def _():
        state_ref[...] = jnp.zeros_like(state_ref)
        ccarry_ref[...] = jnp.zeros_like(ccarry_ref)

    x = x_ref[0]
    hn = _rmsnorm(x, nw_ref[...]).astype(BF16)

    def conv_cols(c0, width):
        xb = _dot(hn, win_ref[:, inner + c0:inner + c0 + width])
        xe = jnp.concatenate([ccarry_ref[:, c0:c0 + width], xb], axis=0)
        ccarry_ref[:, c0:c0 + width] = xb[tile - CONV_CARRY:]
        x1 = pltpu.roll(xe, 1, 0)
        w0, w1, w2, w3 = (cw_ref[kk:kk + 1, c0:c0 + width] for kk in range(SSD_CONV))
        acc = (xe * w3 + x1 * w2) + pltpu.roll(xe * w1 + x1 * w0, 2, 0)
        act_ref[:, c0:c0 + width] = _silu(acc[CONV_CARRY:] + cb_ref[:, c0:c0 + width])

    def z_cols(c0, width):
        z_ref[:, c0:c0 + width] = _dot(hn, win_ref[:, c0:c0 + width])

    def project_group_items(g):
        items = []
        for c0 in range(g * gw, (g + 1) * gw, PIECE_COLS):
            items.append(functools.partial(conv_cols, c0, PIECE_COLS))
            items.append(functools.partial(z_cols, c0, PIECE_COLS))
        return items

    def finish_group_items(g):
        g0 = g * gw
        cell = []

        def gate():
            yg = y_ref[:, g0:g0 + gw] + act_ref[:, g0:g0 + gw] * dexp_ref[:, g0:g0 + gw]
            yg = yg * _silu(z_ref[:, g0:g0 + gw])
            ms = jnp.mean(yg * yg, axis=-1, keepdims=True)
            cell.append((yg * lax.rsqrt(ms + EPS) * gnw_ref[:, g0:g0 + gw]).astype(BF16))

        def out_proj():
            part = _dot(cell.pop(), wout_ref[g0:g0 + gw, :])
            if g == 0:
                o_ref[0] = x + part
            else:
                o_ref[0] = o_ref[0] + part

        return [gate, out_proj]

    fill = [functools.partial(conv_cols, c0, PIECE_COLS) for c0 in range(inner, conv_dim, PIECE_COLS)]
    fill += project_group_items(0)

    def emit_fill(n):
        for _ in range(n):
            if fill:
                fill.pop(0)()

    dtr = _dot(hn, win_ref[:, inner + conv_dim:])
    emit_fill(1)

    a_neg = -jnp.exp(alog_ref[...])

    li = lax.broadcasted_iota(jnp.int32, (L, L), 0)
    si = lax.broadcasted_iota(jnp.int32, (L, L), 1)
    causal = si <= li
    tril = jnp.where(causal, 1.0, 0.0).astype(BF16)
    plane = lax.broadcasted_iota(jnp.int32, (L, LANES), 1)
    p_lo = plane < SSD_HEAD_DIM

    chunks = []
    for c in range(nchunk):
        r0 = c * L
        dt = _softplus(dtr[r0:r0 + L] + dtb_ref[...])
        a_cs = sum(_dot(tril, part) for part in _split3(dt * a_neg))
        emit_fill(1)
        a2 = a_cs * LOG2E
        w = dt * jnp.exp(a_cs[L - 1:L, :] - a_cs)
        shift = a2 - jnp.log2(dt)
        chunks.append(dict(a2=a2, wT=w.T[:heads], a2_shiftT=shift.T[:heads]))
        emit_fill(1)
    emit_fill(len(fill))

    pending = []
    for g in range(SSD_GROUPS):
        if g + 1 < SSD_GROUPS:
            pending.extend(project_group_items(g + 1))
        g0 = g * gw
        for c, ck in enumerate(chunks):
            r0 = c * L
            bg = act_ref[r0:r0 + L, inner + g * SSD_STATE:inner + (g + 1) * SSD_STATE]
            cg = act_ref[r0:r0 + L, inner + gn + g * SSD_STATE:inner + gn + (g + 1) * SSD_STATE].astype(BF16)
            bgT = bg.T.astype(BF16)
            cb = _dot(cg, bgT).astype(BF16)
            y_off = _dot(cg, state_ref[:, g0:g0 + gw].astype(BF16))
            for pr in range(pairs_per_group):
                h0 = g * heads_per_group + 2 * pr
                c0 = h0 * SSD_HEAD_DIM
                xp = act_ref[r0:r0 + L, c0:c0 + LANES]
                w_a = jnp.concatenate([jnp.where(p_lo, xp, 0.0), jnp.where(p_lo, 0.0, xp)], axis=0).astype(BF16)
                m_parts, b_parts, e_parts = [], [], []
                for hd in (h0, h0 + 1):
                    acol = jnp.broadcast_to(ck["a2"][:, hd:hd + 1], (L, L))
                    dec = jnp.exp2(jnp.where(causal, acol - ck["a2_shiftT"][hd:hd + 1, :], NEG))
                    m_parts.append(cb * dec.astype(BF16))
                    b_parts.append(bgT * ck["wT"][hd:hd + 1, :].astype(BF16))
                    e_parts.append(jnp.exp2(acol))
                lhs = jnp.concatenate([jnp.concatenate(m_parts, axis=1),
                                       jnp.concatenate(b_parts, axis=1)], axis=0)
                ra = _dot(lhs, w_a)
                ea_pair = jnp.where(p_lo, e_parts[0], e_parts[1])
                y_ref[r0:r0 + L, c0:c0 + LANES] = ra[:L] + y_off[:, c0 - g0:c0 - g0 + LANES] * ea_pair
                state_ref[:, c0:c0 + LANES] = state_ref[:, c0:c0 + LANES] * ea_pair[L - 1:L, :] + ra[L:]
                if pending:
                    pending.pop(0)()
        while pending:
            pending.pop(0)()
        pending.extend(finish_group_items(g))
    while pending:
        pending.pop(0)()


def _odd_mixer(x, nw, w_in, cw, cb, dtb, alog, dexp, gnw, w_out, *, heads, tile, casts=()):
    b, s, d = x.shape
    inner = heads * SSD_HEAD_DIM
    conv_dim = inner + 2 * SSD_GROUPS * SSD_STATE
    operands = (x, nw, w_in, cw, cb, dtb, alog, dexp, gnw, w_out)
    return _fused_call(
        functools.partial(_odd_kernel, tile=tile, inner=inner, heads=heads),
        name="odd_mixer",
        grid=(b, s // tile),
        in_specs=[pl.BlockSpec((1, tile, d), lambda i, j: (i, j, 0))]
        + [_const_spec(c.shape) for c in operands[1:]],
        operands=operands,
        out_spec=pl.BlockSpec((1, tile, d), lambda i, j: (i, j, 0)),
        out_shape=jax.ShapeDtypeStruct((b, s, d), F32),
        scratch_shapes=[
            pltpu.VMEM((SSD_STATE, inner), F32),
            pltpu.VMEM((CONV_CARRY, conv_dim), F32),
            pltpu.VMEM((tile, conv_dim), F32),
            pltpu.VMEM((tile, inner), F32),
            pltpu.VMEM((tile, inner), F32),
        ],
        casts=casts,
    )


def _row(v):
    return v.reshape(1, -1).astype(F32)


def kernel(x, mix_norm_w, ap_w_in, ap_sinks, pool_w, pool_scale, ap_w_out, ssd_w_in, ssd_conv_w, ssd_conv_b, ssd_dt_bias, ssd_A_log, ssd_D, ssd_norm_w, ssd_w_out, ffn_norm_w, w_gate, w_up, w_down, final_norm_w):
    b, s, d = x.shape
    depth = mix_norm_w.shape[0]
    heads = ssd_dt_bias.shape[1]
    inner = heads * SSD_HEAD_DIM
    conv_dim = inner + 2 * SSD_GROUPS * SSD_STATE
    even_tile = min(EVEN_TILE, s)
    odd_tile = min(ODD_TILE, s)
    ffn_tile = min(FFN_TILE, b * s)

    ap_w_in_b, ap_w_out_b = _to_bf16(ap_w_in), _to_bf16(ap_w_out)
    ssd_w_in_t = jnp.swapaxes(ssd_w_in, 1, 2)

    def ffn_casts(layer):
        return (_Cast(w_gate, layer), _Cast(w_up, layer), _Cast(w_down, layer))

    def ssd_casts(i):
        return (_Cast(ssd_w_in_t, i, transpose=True), _Cast(ssd_w_out, i))

    def lane_pad(a):
        return jnp.pad(a, ((0, 0), (0, LANES - heads)))

    for layer in range(depth):
        i = layer // 2
        nw = _row(mix_norm_w[layer])
        if layer % 2 == 0:
            sink_rep = jnp.broadcast_to(
                (ap_sinks[i].astype(F32) * LOG2E).reshape(ATTN_KV_HEADS, ATTN_GROUP, 1, 1),
                (ATTN_KV_HEADS, ATTN_GROUP, ATTN_BLOCK, LANES)).reshape(ATTN_KV_HEADS, ATTN_GROUP * ATTN_BLOCK, LANES)
            x, ffn_w = _even_mixer(x, nw, ap_w_in_b, sink_rep, pool_w[i].astype(BF16), _row(pool_scale[i]),
                                   ap_w_out_b, i, tile=even_tile, casts=ffn_casts(layer))
        else:
            x, ffn_w = _odd_mixer(x, nw, ssd_w[0], ssd_conv_w[i].astype(F32), _row(ssd_conv_b[i]),
                                  lane_pad(_row(ssd_dt_bias[i])), lane_pad(_row(ssd_A_log[i])),
                                  _row(jnp.repeat(ssd_D[i], SSD_HEAD_DIM)), _row(ssd_norm_w[i]),
                                  ssd_w[1], heads=heads, tile=odd_tile, casts=ffn_casts(layer))
        final = layer == depth - 1
        next_is_odd = not final and (layer + 1) % 2 == 1
        x2d, ssd_w = _ffn(x.reshape(b * s, d), _row(ffn_norm_w[layer]), *ffn_w, _row(final_norm_w),
                          final=final, tile=ffn_tile, casts=ssd_casts((layer + 1) // 2) if next_is_odd else ())
        x = x2d.reshape(b, s, d)
    return x
```

```python
import functools
from typing import NamedTuple

import jax
import jax.numpy as jnp
from jax import lax
from jax.experimental import pallas as pl
from jax.experimental.pallas import tpu as pltpu

F32 = jnp.float32
BF16 = jnp.bfloat16

EPS = 1e-5
LANES = 128
BF16_SUBLANES = 16
PIECE_COLS = 256
ATTN_HEADS = 8
ATTN_KV_HEADS = 2
HEAD_DIM = 64
ATTN_BLOCK = 128
ATTN_GROUP = ATTN_HEADS // ATTN_KV_HEADS
ATTN_WIDTH = ATTN_HEADS * HEAD_DIM
KV_WIDTH = ATTN_KV_HEADS * HEAD_DIM
POOL_WINDOWS = (2, 4, 8, 16)
POOL_GROUP_DIM = 128
POOL_WIDTH = len(POOL_WINDOWS) * POOL_GROUP_DIM
POOL_CARRY = 16
SSD_HEAD_DIM = 64
SSD_GROUPS = 4
SSD_STATE = 128
SSD_CONV = 4
SSD_CHUNK = 128
CONV_CARRY = 8
assert SSD_CONV == 4
NEG = -1e30
LOG2E = 1.4426950408889634

VMEM_LIMIT = 56 * 1024 * 1024
CAST_BLOCK_BYTES = 6 * 1024 * 1024

EVEN_TILE = 1024
ODD_TILE = 512
FFN_TILE = 1024
FFN_HIDDEN_CHUNK = 256


def _dot(a, b):
    return jnp.dot(a, b, preferred_element_type=F32)


def _dot_nt(a, b):
    return lax.dot_general(a, b, (((1,), (1,)), ((), ())), preferred_element_type=F32)


def _rmsnorm(x, w):
    ms = jnp.mean(x * x, axis=-1, keepdims=True)
    return x * lax.rsqrt(ms + EPS) * w


def _silu(x):
    h = 0.5 * x
    return h + h * jnp.tanh(h)


def _split3(v):
    hi = v.astype(BF16)
    r = v - hi.astype(F32)
    mid = r.astype(BF16)
    lo = (r - mid.astype(F32)).astype(BF16)
    return hi, mid, lo


def _const_spec(shape):
    nd = len(shape)
    return pl.BlockSpec(shape, lambda *_: (0,) * nd, pipeline_mode=pl.Buffered(1))


def _layer_spec(w, layer):
    return pl.BlockSpec((None,) + w.shape[1:], lambda *_: (layer, 0, 0), pipeline_mode=pl.Buffered(1))


def _cast_kernel(x_ref, o_ref, *, cols):
    aligned = cols // LANES * LANES
    o_ref[0, :, :aligned] = x_ref[0, :, :aligned].astype(BF16)
    if aligned < cols:
        o_ref[0, :, aligned:] = jnp.zeros((o_ref.shape[1], o_ref.shape[2] - aligned), BF16)
        o_ref[0, :, aligned:cols] = x_ref[0, :, aligned:].astype(BF16)


def _to_bf16(w):
    nl, r, c = w.shape
    c_out = -(-c // LANES) * LANES
    rb = r
    while rb * c * 4 > CAST_BLOCK_BYTES and rb % 32 == 0:
        rb //= 2
    return pl.pallas_call(
        functools.partial(_cast_kernel, cols=c),
        grid=(nl, r // rb),
        in_specs=[pl.BlockSpec((1, rb, c), lambda i, j: (i, j, 0))],
        out_specs=pl.BlockSpec((1, rb, c_out), lambda i, j: (i, j, 0)),
        out_shape=jax.ShapeDtypeStruct((nl, r, c_out), BF16),
        compiler_params=pltpu.CompilerParams(
            dimension_semantics=("arbitrary", "arbitrary"), vmem_limit_bytes=VMEM_LIMIT),
        name="cast_bf16",
    )(w)


class _Cast(NamedTuple):
    w: jax.Array
    layer: int
    transpose: bool = False


def _cast_plan(cast, nsteps):
    if cast.transpose:
        _, c, r = cast.w.shape
        nstrips = max(k for k in range(1, nsteps + 1) if nsteps % k == 0 and (r // LANES) % k == 0)
        c_out = -(-c // LANES) * LANES
        plan = ((None, c, r // nstrips), lambda s: (cast.layer, 0, s), (r // nstrips, c_out), (r, c_out))
    else:
        _, r, c = cast.w.shape
        nstrips = max(k for k in range(1, nsteps + 1)
                      if nsteps % k == 0 and r % k == 0 and (r // k) % BF16_SUBLANES == 0)
        rb = r // nstrips
        plan = ((None, rb, c), lambda s: (cast.layer, s, 0), (rb, c), (r, c))
    assert nsteps % nstrips == 0
    return plan + (nsteps // nstrips,)


def _emit_cast(transpose, reps, step, i_ref, o_ref):
    if not transpose:
        o_ref[...] = i_ref[...].astype(BF16)
        return

    @pl.when(step % reps == 0)
    def _():
        c, width = i_ref.shape
        aligned = c // LANES * LANES
        xt = i_ref[...]
        parts = [xt[:aligned].T]
        if aligned < c:
            tail = jnp.concatenate([xt[aligned:], jnp.zeros((LANES - (c - aligned), width), F32)], axis=0)
            parts.append(tail.T)
        o_ref[...] = jnp.concatenate(parts, axis=1).astype(BF16)


def _fused_kernel(*refs, body, n_in, cast_meta, grid):
    n_c = len(cast_meta)
    ins, cast_ins = refs[:n_in], refs[n_in:n_in + n_c]
    out, cast_outs = refs[n_in + n_c], refs[n_in + n_c + 1:n_in + 2 * n_c + 1]
    scratch = refs[n_in + 2 * n_c + 1:]
    body(*ins, out, *scratch)
    step = pl.program_id(0)
    for axis in range(1, len(grid)):
        step = step * grid[axis] + pl.program_id(axis)
    for (transpose, reps), i_ref, o_ref in zip(cast_meta, cast_ins, cast_outs):
        _emit_cast(transpose, reps, step, i_ref, o_ref)


def _fused_call(body, *, name, grid, in_specs, operands, out_spec, out_shape, scratch_shapes=(), casts=()):
    nsteps = 1
    for g in grid:
        nsteps *= g

    def flat(idx):
        s = idx[0]
        for axis in range(1, len(grid)):
            s = s * grid[axis] + idx[axis]
        return s

    cast_in_specs, cast_out_specs, cast_out_shapes, cast_meta = [], [], [], []
    for cast in casts:
        in_block, in_index, out_block, o_shape, reps = _cast_plan(cast, nsteps)
        cast_in_specs.append(pl.BlockSpec(in_block, lambda *g, f=in_index, r=reps: f(flat(g) // r)))
        cast_out_specs.append(pl.BlockSpec(out_block, lambda *g, r=reps: (flat(g) // r, 0)))
        cast_out_shapes.append(jax.ShapeDtypeStruct(o_shape, BF16))
        cast_meta.append((cast.transpose, reps))
    outs = pl.pallas_call(
        functools.partial(_fused_kernel, body=body, n_in=len(in_specs), cast_meta=tuple(cast_meta), grid=grid),
        grid=grid,
        in_specs=list(in_specs) + cast_in_specs,
        out_specs=[out_spec] + cast_out_specs,
        out_shape=[out_shape] + cast_out_shapes,
        scratch_shapes=list(scratch_shapes),
        compiler_params=pltpu.CompilerParams(
            dimension_semantics=("arbitrary",) * len(grid), vmem_limit_bytes=VMEM_LIMIT),
        name=name,
    )(*operands, *[c.w for c in casts])
    return outs[0], list(outs[1:])


def _weight_spec(w, layer):
    return _const_spec(w.shape) if w.ndim == 2 else _layer_spec(w, layer)


def _ffn_kernel(x_ref, nw_ref, wg_ref, wu_ref, wd_ref, fw_ref, o_ref, *, final):
    x = x_ref[...]
    h = _rmsnorm(x, nw_ref[...]).astype(BF16)
    y = x
    for c0 in range(0, wg_ref.shape[1], FFN_HIDDEN_CHUNK):
        cols = slice(c0, c0 + FFN_HIDDEN_CHUNK)
        g = _dot(h, wg_ref[:, cols])
        u = _dot(h, wu_ref[:, cols])
        a = (_silu(g) * u).astype(BF16)
        y = y + _dot(a, wd_ref[cols, :])
    if final:
        y = _rmsnorm(y, fw_ref[...])
    o_ref[...] = y


def _ffn(x2d, nw, wg, wu, wd, fw, *, final, tile):
    n, d = x2d.shape
    return _fused_call(
        functools.partial(_ffn_kernel, final=final),
        name="ffn",
        grid=(n // tile,),
        in_specs=[
            pl.BlockSpec((tile, d), lambda i: (i, 0)),
            _const_spec((1, d)),
            _const_spec(wg.shape),
            _const_spec(wu.shape),
            _const_spec(wd.shape),
            _const_spec((1, d)),
        ],
        operands=(x2d, nw, wg, wu, wd, fw),
        out_spec=pl.BlockSpec((tile, d), lambda i: (i, 0)),
        out_shape=jax.ShapeDtypeStruct((n, d), F32),
    )[0]


def _even_kernel(x_ref, nw_ref, win_ref, sink_ref, pw_ref, ps_ref, wout_ref, o_ref,
                 kprev_ref, vprev_ref, pcarry_ref, attn_ref, *, tile):
    t = pl.program_id(1)
    nblk = tile // ATTN_BLOCK
    blk = ATTN_BLOCK

    @pl.when(t == 0)
    def _():
        kprev_ref[...] = jnp.zeros_like(kprev_ref)
        vprev_ref[...] = jnp.zeros_like(vprev_ref)
        pcarry_ref[...] = jnp.zeros_like(pcarry_ref)

    x = x_ref[0]
    h = _rmsnorm(x, nw_ref[...]).astype(BF16)
    proj = _dot(h, win_ref[...])
    q = proj[:, :ATTN_WIDTH] * (HEAD_DIM ** -0.5 * LOG2E)
    k = proj[:, ATTN_WIDTH:ATTN_WIDTH + KV_WIDTH]
    v = proj[:, ATTN_WIDTH + KV_WIDTH:ATTN_WIDTH + 2 * KV_WIDTH]
    p = proj[:, ATTN_WIDTH + 2 * KV_WIDTH:]
    q_is_even = lax.broadcasted_iota(jnp.int32, q.shape, 1) % LANES < HEAD_DIM
    q_par = (jnp.where(q_is_even, q, 0.0).astype(BF16), jnp.where(q_is_even, 0.0, q).astype(BF16))

    lane = lax.broadcasted_iota(jnp.int32, (tile, KV_WIDTH), 1)
    lo_half = lane < HEAD_DIM

    def both_halves(a):
        sw = pltpu.roll(a, HEAD_DIM, 1)
        return (jnp.where(lo_half, a, sw).astype(BF16), jnp.where(lo_half, sw, a).astype(BF16))

    k2 = both_halves(k)
    v2 = both_halves(v)
    kfull = [jnp.concatenate([kprev_ref[hh], k2[hh]], axis=0) for hh in range(ATTN_KV_HEADS)]
    vfull = [jnp.concatenate([vprev_ref[hh], v2[hh]], axis=0) for hh in range(ATTN_KV_HEADS)]
    for hh in range(ATTN_KV_HEADS):
        kprev_ref[hh] = k2[hh][tile - blk:]
        vprev_ref[hh] = v2[hh][tile - blk:]

    rows = ATTN_GROUP * blk
    qi = lax.broadcasted_iota(jnp.int32, (rows, 2 * blk), 0) % blk
    kj = lax.broadcasted_iota(jnp.int32, (rows, 2 * blk), 1)
    band = (kj > qi) & (kj <= qi + blk)
    first_key = jnp.where(t == 0, blk, 0)
    qlane = lax.broadcasted_iota(jnp.int32, (blk, LANES), 1)
    q_lo = qlane < HEAD_DIM
    ones = jnp.ones((2 * blk, LANES), BF16)

    for j in range(nblk):
        valid = band & (kj >= first_key) if j == 0 else band
        for hh in range(ATTN_KV_HEADS):
            parts = []
            for g in range(ATTN_GROUP):
                c0 = (hh * ATTN_GROUP + g) // 2 * LANES
                parts.append(q_par[g % 2][j * blk:(j + 1) * blk, c0:c0 + LANES])
            qst = jnp.concatenate(parts, axis=0)
            kb = kfull[hh][j * blk:(j + 2) * blk]
            vb = jnp.concatenate([vfull[hh][j * blk:(j + 2) * blk], ones], axis=1)
            s = _dot_nt(qst, kb)
            s = jnp.where(valid, s, NEG)
            sink = sink_ref[hh]
            m = jnp.maximum(jnp.max(s, axis=-1, keepdims=True), sink)
            e = jnp.exp2(s - jnp.concatenate([m, m], axis=1))
            r = _dot(e.astype(BF16), vb)
            den = r[:, LANES:] + jnp.exp2(sink - m)
            o = r[:, :LANES] * (1.0 / den)
            for a in range(ATTN_GROUP // 2):
                slab = jnp.where(q_lo, o[(2 * a) * blk:(2 * a + 1) * blk],
                                 o[(2 * a + 1) * blk:(2 * a + 2) * blk])
                c0 = (hh * ATTN_GROUP // 2 + a) * LANES
                attn_ref[j * blk:(j + 1) * blk, c0:c0 + LANES] = slab.astype(BF16)

    pe = jnp.concatenate([pcarry_ref[...], p], axis=0)
    pcarry_ref[...] = p[tile - POOL_CARRY:]
    cnt = (t * tile + 1 + lax.broadcasted_iota(jnp.int32, (tile, POOL_GROUP_DIM), 0)).astype(F32)
    pooled = []
    for gi, w in enumerate(POOL_WINDOWS):
        u = pe[:, gi * POOL_GROUP_DIM:(gi + 1) * POOL_GROUP_DIM]
        sm = u
        sh = 1
        while sh < w:
            sm = sm + pltpu.roll(sm, sh, 0)
            sh *= 2
        mean = sm[POOL_CARRY:] / jnp.minimum(cnt, float(w))
        d = (mean - u[POOL_CARRY:]).astype(BF16)
        yg = _dot(d, pw_ref[gi]) * ps_ref[:, gi * POOL_GROUP_DIM:(gi + 1) * POOL_GROUP_DIM]
        pooled.append(yg.astype(BF16))
    mixed = jnp.concatenate(pooled, axis=1)
    o_ref[0] = (x + _dot(attn_ref[...], wout_ref[:ATTN_WIDTH, :])
                + _dot(mixed, wout_ref[ATTN_WIDTH:, :]))


def _even_mixer(x, nw, w_in, sink_rep, pw, ps, w_out, layer, *, tile, casts=()):
    b, s, d = x.shape
    return _fused_call(
        functools.partial(_even_kernel, tile=tile),
        name="even_mixer",
        grid=(b, s // tile),
        in_specs=[
            pl.BlockSpec((1, tile, d), lambda i, j: (i, j, 0)),
            _const_spec(nw.shape),
            _weight_spec(w_in, layer),
            _const_spec(sink_rep.shape),
            _const_spec(pw.shape),
            _const_spec(ps.shape),
            _weight_spec(w_out, layer),
        ],
        operands=(x, nw, w_in, sink_rep, pw, ps, w_out),
        out_spec=pl.BlockSpec((1, tile, d), lambda i, j: (i, j, 0)),
        out_shape=jax.ShapeDtypeStruct((b, s, d), F32),
        scratch_shapes=[
            pltpu.VMEM((ATTN_KV_HEADS, ATTN_BLOCK, KV_WIDTH), BF16),
            pltpu.VMEM((ATTN_KV_HEADS, ATTN_BLOCK, KV_WIDTH), BF16),
            pltpu.VMEM((POOL_CARRY, POOL_WIDTH), F32),
            pltpu.VMEM((tile, ATTN_WIDTH), BF16),
        ],
        casts=casts,
    )


def _softplus(x):
    return jnp.maximum(x, 0.0) + jnp.log(1.0 + jnp.exp(-jnp.abs(x)))


def _odd_kernel(x_ref, nw_ref, win_ref, cw_ref, cb_ref, dtb_ref, alog_ref, dexp_ref, gnw_ref, wout_ref, o_ref,
                state_ref, ccarry_ref, act_ref, z_ref, y_ref, *, tile, inner, heads):
    t = pl.program_id(1)
    L = SSD_CHUNK
    nchunk = tile // L
    gn = SSD_GROUPS * SSD_STATE
    conv_dim = inner + 2 * gn
    heads_per_group = heads // SSD_GROUPS
    pairs_per_group = heads_per_group // 2
    gw = heads_per_group * SSD_HEAD_DIM

    @pl.when(t == 0)
    def _():
        state_ref[...] = jnp.zeros_like(state_ref)
        ccarry_ref[...] = jnp.zeros_like(ccarry_ref)

    x = x_ref[0]
    hn = _rmsnorm(x, nw_ref[...]).astype(BF16)

    def conv_cols(c0, width):
        xb = _dot(hn, win_ref[:, inner + c0:inner + c0 + width])
        xe = jnp.concatenate([ccarry_ref[:, c0:c0 + width], xb], axis=0)
        ccarry_ref[:, c0:c0 + width] = xb[tile - CONV_CARRY:]
        x1 = pltpu.roll(xe, 1, 0)
        w0, w1, w2, w3 = (cw_ref[kk:kk + 1, c0:c0 + width] for kk in range(SSD_CONV))
        acc = (xe * w3 + x1 * w2) + pltpu.roll(xe * w1 + x1 * w0, 2, 0)
        act_ref[:, c0:c0 + width] = _silu(acc[CONV_CARRY:] + cb_ref[:, c0:c0 + width])

    def z_cols(c0, width):
        z_ref[:, c0:c0 + width] = _dot(hn, win_ref[:, c0:c0 + width])

    def project_group_items(g):
        items = []
        for c0 in range(g * gw, (g + 1) * gw, PIECE_COLS):
            items.append(functools.partial(conv_cols, c0, PIECE_COLS))
            items.append(functools.partial(z_cols, c0, PIECE_COLS))
        return items

    def finish_group_items(g):
        g0 = g * gw
        cell = []

        def gate():
            yg = y_ref[:, g0:g0 + gw] + act_ref[:, g0:g0 + gw] * dexp_ref[:, g0:g0 + gw]
            yg = yg * _silu(z_ref[:, g0:g0 + gw])
            ms = jnp.mean(yg * yg, axis=-1, keepdims=True)
            cell.append((yg * lax.rsqrt(ms + EPS) * gnw_ref[:, g0:g0 + gw]).astype(BF16))

        def out_proj():
            part = _dot(cell.pop(), wout_ref[g0:g0 + gw, :])
            if g == 0:
                o_ref[0] = x + part
            else:
                o_ref[0] = o_ref[0] + part

        return [gate, out_proj]

    fill = [functools.partial(conv_cols, c0, PIECE_COLS) for c0 in range(inner, conv_dim, PIECE_COLS)]
    fill += project_group_items(0)

    def emit_fill(n):
        for _ in range(n):
            if fill:
                fill.pop(0)()

    dtr = _dot(hn, win_ref[:, inner + conv_dim:])
    emit_fill(1)

    a_neg = -jnp.exp(alog_ref[...])

    li = lax.broadcasted_iota(jnp.int32, (L, L), 0)
    si = lax.broadcasted_iota(jnp.int32, (L, L), 1)
    causal = si <= li
    tril = jnp.where(causal, 1.0, 0.0).astype(BF16)
    plane = lax.broadcasted_iota(jnp.int32, (L, LANES), 1)
    p_lo = plane < SSD_HEAD_DIM

    chunks = []
    for c in range(nchunk):
        r0 = c * L
        dt = _softplus(dtr[r0:r0 + L] + dtb_ref[...])
        a_cs = sum(_dot(tril, part) for part in _split3(dt * a_neg))
        emit_fill(1)
        a2 = a_cs * LOG2E
        w = dt * jnp.exp(a_cs[L - 1:L, :] - a_cs)
        shift = a2 - jnp.log2(dt)
        chunks.append(dict(a2=a2, wT=w.T[:heads], a2_shiftT=shift.T[:heads]))
        emit_fill(1)
    emit_fill(len(fill))

    pending = []
    pairs_per_step = nchunk * pairs_per_group
    for g in range(SSD_GROUPS):
        if g + 1 < SSD_GROUPS:
            pending.extend(project_group_items(g + 1))
        n_items, n_pairs = len(pending), 0
        g0 = g * gw
        for c, ck in enumerate(chunks):
            r0 = c * L
            bg = act_ref[r0:r0 + L, inner + g * SSD_STATE:inner + (g + 1) * SSD_STATE]
            cg = act_ref[r0:r0 + L, inner + gn + g * SSD_STATE:inner + gn + (g + 1) * SSD_STATE].astype(BF16)
            bgT = bg.T.astype(BF16)
            cb = _dot(cg, bgT).astype(BF16)
            y_off = _dot(cg, state_ref[:, g0:g0 + gw].astype(BF16))
            for pr in range(pairs_per_group):
                h0 = g * heads_per_group + 2 * pr
                c0 = h0 * SSD_HEAD_DIM
                xp = act_ref[r0:r0 + L, c0:c0 + LANES]
                w_a = jnp.concatenate([jnp.where(p_lo, xp, 0.0), jnp.where(p_lo, 0.0, xp)], axis=0).astype(BF16)
                m_parts, b_parts, e_parts = [], [], []
                for hd in (h0, h0 + 1):
                    acol = jnp.broadcast_to(ck["a2"][:, hd:hd + 1], (L, L))
                    dec = jnp.exp2(jnp.where(causal, acol - ck["a2_shiftT"][hd:hd + 1, :], NEG))
                    m_parts.append(cb * dec.astype(BF16))
                    b_parts.append(bgT * ck["wT"][hd:hd + 1, :].astype(BF16))
                    e_parts.append(jnp.exp2(acol))
                lhs = jnp.concatenate([jnp.concatenate(m_parts, axis=1),
                                       jnp.concatenate(b_parts, axis=1)], axis=0)
                ra = _dot(lhs, w_a)
                ea_pair = jnp.where(p_lo, e_parts[0], e_parts[1])
                y_ref[r0:r0 + L, c0:c0 + LANES] = ra[:L] + y_off[:, c0 - g0:c0 - g0 + LANES] * ea_pair
                state_ref[:, c0:c0 + LANES] = state_ref[:, c0:c0 + LANES] * ea_pair[L - 1:L, :] + ra[L:]
                n_pairs += 1
                while len(pending) > n_items - n_pairs * n_items // pairs_per_step:
                    pending.pop(0)()
        pending.extend(finish_group_items(g))
    while pending:
        pending.pop(0)()


def _odd_mixer(x, nw, w_in, cw, cb, dtb, alog, dexp, gnw, w_out, *, heads, tile, casts=()):
    b, s, d = x.shape
    inner = heads * SSD_HEAD_DIM
    conv_dim = inner + 2 * SSD_GROUPS * SSD_STATE
    operands = (x, nw, w_in, cw, cb, dtb, alog, dexp, gnw, w_out)
    return _fused_call(
        functools.partial(_odd_kernel, tile=tile, inner=inner, heads=heads),
        name="odd_mixer",
        grid=(b, s // tile),
        in_specs=[pl.BlockSpec((1, tile, d), lambda i, j: (i, j, 0))]
        + [_const_spec(c.shape) for c in operands[1:]],
        operands=operands,
        out_spec=pl.BlockSpec((1, tile, d), lambda i, j: (i, j, 0)),
        out_shape=jax.ShapeDtypeStruct((b, s, d), F32),
        scratch_shapes=[
            pltpu.VMEM((SSD_STATE, inner), F32),
            pltpu.VMEM((CONV_CARRY, conv_dim), F32),
            pltpu.VMEM((tile, conv_dim), F32),
            pltpu.VMEM((tile, inner), F32),
            pltpu.VMEM((tile, inner), F32),
        ],
        casts=casts,
    )


def _row(v):
    return v.reshape(1, -1).astype(F32)


def kernel(x, mix_norm_w, ap_w_in, ap_sinks, pool_w, pool_scale, ap_w_out, ssd_w_in, ssd_conv_w, ssd_conv_b, ssd_dt_bias, ssd_A_log, ssd_D, ssd_norm_w, ssd_w_out, ffn_norm_w, w_gate, w_up, w_down, final_norm_w):
    b, s, d = x.shape
    depth = mix_norm_w.shape[0]
    heads = ssd_dt_bias.shape[1]
    inner = heads * SSD_HEAD_DIM
    conv_dim = inner + 2 * SSD_GROUPS * SSD_STATE
    even_tile = min(EVEN_TILE, s)
    odd_tile = min(ODD_TILE, s)
    ffn_tile = min(FFN_TILE, b * s)

    ap_w_in_b, ap_w_out_b = _to_bf16(ap_w_in), _to_bf16(ap_w_out)
    ssd_w_in_t = jnp.swapaxes(ssd_w_in, 1, 2)

    def ffn_casts(layer):
        return (_Cast(w_gate, layer), _Cast(w_up, layer), _Cast(w_down, layer))

    def ssd_casts(i):
        return (_Cast(ssd_w_in_t, i, transpose=True), _Cast(ssd_w_out, i))

    def lane_pad(a):
        return jnp.pad(a, ((0, 0), (0, LANES - heads)))

    for layer in range(depth):
        i = layer // 2
        nw = _row(mix_norm_w[layer])
        if layer % 2 == 0:
            sink_rep = jnp.broadcast_to(
                (ap_sinks[i].astype(F32) * LOG2E).reshape(ATTN_KV_HEADS, ATTN_GROUP, 1, 1),
                (ATTN_KV_HEADS, ATTN_GROUP, ATTN_BLOCK, LANES)).reshape(ATTN_KV_HEADS, ATTN_GROUP * ATTN_BLOCK, LANES)
            carried = ffn_casts(layer) + (ssd_casts(i) if layer + 1 < depth else ())
            x, ready = _even_mixer(x, nw, ap_w_in_b, sink_rep, pool_w[i].astype(BF16), _row(pool_scale[i]),
                                   ap_w_out_b, i, tile=even_tile, casts=carried)
            ffn_w, ssd_w = ready[:3], ready[3:]
        else:
            x, ffn_w = _odd_mixer(x, nw, ssd_w[0], ssd_conv_w[i].astype(F32), _row(ssd_conv_b[i]),
                                  lane_pad(_row(ssd_dt_bias[i])), lane_pad(_row(ssd_A_log[i])),
                                  _row(jnp.repeat(ssd_D[i], SSD_HEAD_DIM)), _row(ssd_norm_w[i]),
                                  ssd_w[1], heads=heads, tile=odd_tile, casts=ffn_casts(layer))
        x = _ffn(x.reshape(b * s, d), _row(ffn_norm_w[layer]), *ffn_w, _row(final_norm_w),
                 final=layer == depth - 1, tile=ffn_tile).reshape(b, s, d)
    return x
```

```python
import functools
from typing import NamedTuple

import jax
import jax.numpy as jnp
from jax import lax
from jax.experimental import pallas as pl
from jax.experimental.pallas import tpu as pltpu

F32 = jnp.float32
BF16 = jnp.bfloat16

EPS = 1e-5
LANES = 128
BF16_SUBLANES = 16
PIECE_COLS = 256
ATTN_HEADS = 8
ATTN_KV_HEADS = 2
HEAD_DIM = 64
ATTN_BLOCK = 128
ATTN_GROUP = ATTN_HEADS // ATTN_KV_HEADS
ATTN_WIDTH = ATTN_HEADS * HEAD_DIM
KV_WIDTH = ATTN_KV_HEADS * HEAD_DIM
POOL_WINDOWS = (2, 4, 8, 16)
POOL_GROUP_DIM = 128
POOL_WIDTH = len(POOL_WINDOWS) * POOL_GROUP_DIM
POOL_CARRY = 16
SSD_HEAD_DIM = 64
SSD_GROUPS = 4
SSD_STATE = 128
SSD_CONV = 4
SSD_CHUNK = 128
CONV_CARRY = 8
assert SSD_CONV == 4
NEG = -1e30
LOG2E = 1.4426950408889634

VMEM_LIMIT = 56 * 1024 * 1024
CAST_BLOCK_BYTES = 6 * 1024 * 1024

EVEN_TILE = 1024
ODD_TILE = 512
FFN_TILE = 1024
FFN_HIDDEN_CHUNK = 256


def _dot(a, b):
    return jnp.dot(a, b, preferred_element_type=F32)


def _dot_nt(a, b):
    return lax.dot_general(a, b, (((1,), (1,)), ((), ())), preferred_element_type=F32)


def _rmsnorm(x, w):
    ms = jnp.mean(x * x, axis=-1, keepdims=True)
    return x * lax.rsqrt(ms + EPS) * w


def _silu(x):
    h = 0.5 * x
    return h + h * jnp.tanh(h)


def _split3(v):
    hi = v.astype(BF16)
    r = v - hi.astype(F32)
    mid = r.astype(BF16)
    lo = (r - mid.astype(F32)).astype(BF16)
    return hi, mid, lo


def _const_spec(shape):
    nd = len(shape)
    return pl.BlockSpec(shape, lambda *_: (0,) * nd, pipeline_mode=pl.Buffered(1))


def _layer_spec(w, layer):
    return pl.BlockSpec((None,) + w.shape[1:], lambda *_: (layer, 0, 0), pipeline_mode=pl.Buffered(1))


def _cast_kernel(x_ref, o_ref, *, cols):
    aligned = cols // LANES * LANES
    o_ref[0, :, :aligned] = x_ref[0, :, :aligned].astype(BF16)
    if aligned < cols:
        o_ref[0, :, aligned:] = jnp.zeros((o_ref.shape[1], o_ref.shape[2] - aligned), BF16)
        o_ref[0, :, aligned:cols] = x_ref[0, :, aligned:].astype(BF16)


def _to_bf16(w):
    nl, r, c = w.shape
    c_out = -(-c // LANES) * LANES
    rb = r
    while rb * c * 4 > CAST_BLOCK_BYTES and rb % 32 == 0:
        rb //= 2
    return pl.pallas_call(
        functools.partial(_cast_kernel, cols=c),
        grid=(nl, r // rb),
        in_specs=[pl.BlockSpec((1, rb, c), lambda i, j: (i, j, 0))],
        out_specs=pl.BlockSpec((1, rb, c_out), lambda i, j: (i, j, 0)),
        out_shape=jax.ShapeDtypeStruct((nl, r, c_out), BF16),
        compiler_params=pltpu.CompilerParams(
            dimension_semantics=("arbitrary", "arbitrary"), vmem_limit_bytes=VMEM_LIMIT),
        name="cast_bf16",
    )(w)


class _Cast(NamedTuple):
    w: jax.Array
    layer: int
    transpose: bool = False


def _cast_plan(cast, nsteps):
    if cast.transpose:
        _, c, r = cast.w.shape
        nstrips = max(k for k in range(1, nsteps + 1) if nsteps % k == 0 and (r // LANES) % k == 0)
        c_out = -(-c // LANES) * LANES
        plan = ((None, c, r // nstrips), lambda s: (cast.layer, 0, s), (r // nstrips, c_out), (r, c_out))
    else:
        _, r, c = cast.w.shape
        nstrips = max(k for k in range(1, nsteps + 1)
                      if nsteps % k == 0 and r % k == 0 and (r // k) % BF16_SUBLANES == 0)
        rb = r // nstrips
        plan = ((None, rb, c), lambda s: (cast.layer, s, 0), (rb, c), (r, c))
    assert nsteps % nstrips == 0
    return plan + (nsteps // nstrips,)


def _emit_cast(transpose, reps, step, i_ref, o_ref):
    if not transpose:
        o_ref[...] = i_ref[...].astype(BF16)
        return

    @pl.when(step % reps == 0)
    def _():
        c, width = i_ref.shape
        aligned = c // LANES * LANES
        xt = i_ref[...]
        parts = [xt[:aligned].T]
        if aligned < c:
            tail = jnp.concatenate([xt[aligned:], jnp.zeros((LANES - (c - aligned), width), F32)], axis=0)
            parts.append(tail.T)
        o_ref[...] = jnp.concatenate(parts, axis=1).astype(BF16)


def _fused_kernel(*refs, body, n_in, cast_meta, grid):
    n_c = len(cast_meta)
    ins, cast_ins = refs[:n_in], refs[n_in:n_in + n_c]
    out, cast_outs = refs[n_in + n_c], refs[n_in + n_c + 1:n_in + 2 * n_c + 1]
    scratch = refs[n_in + 2 * n_c + 1:]
    body(*ins, out, *scratch)
    step = pl.program_id(0)
    for axis in range(1, len(grid)):
        step = step * grid[axis] + pl.program_id(axis)
    for (transpose, reps), i_ref, o_ref in zip(cast_meta, cast_ins, cast_outs):
        _emit_cast(transpose, reps, step, i_ref, o_ref)


def _fused_call(body, *, name, grid, in_specs, operands, out_spec, out_shape, scratch_shapes=(), casts=()):
    nsteps = 1
    for g in grid:
        nsteps *= g

    def flat(idx):
        s = idx[0]
        for axis in range(1, len(grid)):
            s = s * grid[axis] + idx[axis]
        return s

    cast_in_specs, cast_out_specs, cast_out_shapes, cast_meta = [], [], [], []
    for cast in casts:
        in_block, in_index, out_block, o_shape, reps = _cast_plan(cast, nsteps)
        cast_in_specs.append(pl.BlockSpec(in_block, lambda *g, f=in_index, r=reps: f(flat(g) // r)))
        cast_out_specs.append(pl.BlockSpec(out_block, lambda *g, r=reps: (flat(g) // r, 0)))
        cast_out_shapes.append(jax.ShapeDtypeStruct(o_shape, BF16))
        cast_meta.append((cast.transpose, reps))
    outs = pl.pallas_call(
        functools.partial(_fused_kernel, body=body, n_in=len(in_specs), cast_meta=tuple(cast_meta), grid=grid),
        grid=grid,
        in_specs=list(in_specs) + cast_in_specs,
        out_specs=[out_spec] + cast_out_specs,
        out_shape=[out_shape] + cast_out_shapes,
        scratch_shapes=list(scratch_shapes),
        compiler_params=pltpu.CompilerParams(
            dimension_semantics=("arbitrary",) * len(grid), vmem_limit_bytes=VMEM_LIMIT),
        name=name,
    )(*operands, *[c.w for c in casts])
    return outs[0], list(outs[1:])


def _weight_spec(w, layer):
    return _const_spec(w.shape) if w.ndim == 2 else _layer_spec(w, layer)


def _ffn_kernel(x_ref, nw_ref, wg_ref, wu_ref, wd_ref, fw_ref, o_ref, *, final):
    x = x_ref[...]
    h = _rmsnorm(x, nw_ref[...]).astype(BF16)
    y = x
    for c0 in range(0, wg_ref.shape[1], FFN_HIDDEN_CHUNK):
        cols = slice(c0, c0 + FFN_HIDDEN_CHUNK)
        g = _dot(h, wg_ref[:, cols])
        u = _dot(h, wu_ref[:, cols])
        a = (_silu(g) * u).astype(BF16)
        y = y + _dot(a, wd_ref[cols, :])
    if final:
        y = _rmsnorm(y, fw_ref[...])
    o_ref[...] = y


def _ffn(x2d, nw, wg, wu, wd, fw, *, final, tile):
    n, d = x2d.shape
    return _fused_call(
        functools.partial(_ffn_kernel, final=final),
        name="ffn",
        grid=(n // tile,),
        in_specs=[
            pl.BlockSpec((tile, d), lambda i: (i, 0)),
            _const_spec((1, d)),
            _const_spec(wg.shape),
            _const_spec(wu.shape),
            _const_spec(wd.shape),
            _const_spec((1, d)),
        ],
        operands=(x2d, nw, wg, wu, wd, fw),
        out_spec=pl.BlockSpec((tile, d), lambda i: (i, 0)),
        out_shape=jax.ShapeDtypeStruct((n, d), F32),
    )[0]


def _even_kernel(x_ref, nw_ref, win_ref, sink_ref, pw_ref, ps_ref, wout_ref, o_ref,
                 kprev_ref, vprev_ref, pcarry_ref, attn_ref, *, tile):
    t = pl.program_id(1)
    nblk = tile // ATTN_BLOCK
    blk = ATTN_BLOCK

    @pl.when(t == 0)
    def _():
        kprev_ref[...] = jnp.zeros_like(kprev_ref)
        vprev_ref[...] = jnp.zeros_like(vprev_ref)
        pcarry_ref[...] = jnp.zeros_like(pcarry_ref)

    x = x_ref[0]
    h = _rmsnorm(x, nw_ref[...]).astype(BF16)
    proj = _dot(h, win_ref[...])
    q = proj[:, :ATTN_WIDTH] * (HEAD_DIM ** -0.5 * LOG2E)
    k = proj[:, ATTN_WIDTH:ATTN_WIDTH + KV_WIDTH]
    v = proj[:, ATTN_WIDTH + KV_WIDTH:ATTN_WIDTH + 2 * KV_WIDTH]
    p = proj[:, ATTN_WIDTH + 2 * KV_WIDTH:]
    q_is_even = lax.broadcasted_iota(jnp.int32, q.shape, 1) % LANES < HEAD_DIM
    q_par = (jnp.where(q_is_even, q, 0.0).astype(BF16), jnp.where(q_is_even, 0.0, q).astype(BF16))

    lane = lax.broadcasted_iota(jnp.int32, (tile, KV_WIDTH), 1)
    lo_half = lane < HEAD_DIM

    def both_halves(a):
        sw = pltpu.roll(a, HEAD_DIM, 1)
        return (jnp.where(lo_half, a, sw).astype(BF16), jnp.where(lo_half, sw, a).astype(BF16))

    k2 = both_halves(k)
    v2 = both_halves(v)
    kfull = [jnp.concatenate([kprev_ref[hh], k2[hh]], axis=0) for hh in range(ATTN_KV_HEADS)]
    vfull = [jnp.concatenate([vprev_ref[hh], v2[hh]], axis=0) for hh in range(ATTN_KV_HEADS)]
    for hh in range(ATTN_KV_HEADS):
        kprev_ref[hh] = k2[hh][tile - blk:]
        vprev_ref[hh] = v2[hh][tile - blk:]

    rows = ATTN_GROUP * blk
    qi = lax.broadcasted_iota(jnp.int32, (rows, 2 * blk), 0) % blk
    kj = lax.broadcasted_iota(jnp.int32, (rows, 2 * blk), 1)
    band = (kj > qi) & (kj <= qi + blk)
    first_key = jnp.where(t == 0, blk, 0)
    qlane = lax.broadcasted_iota(jnp.int32, (blk, LANES), 1)
    q_lo = qlane < HEAD_DIM
    ones = jnp.ones((2 * blk, LANES), BF16)

    for j in range(nblk):
        valid = band & (kj >= first_key) if j == 0 else band
        for hh in range(ATTN_KV_HEADS):
            parts = []
            for g in range(ATTN_GROUP):
                c0 = (hh * ATTN_GROUP + g) // 2 * LANES
                parts.append(q_par[g % 2][j * blk:(j + 1) * blk, c0:c0 + LANES])
            qst = jnp.concatenate(parts, axis=0)
            kb = kfull[hh][j * blk:(j + 2) * blk]
            vb = jnp.concatenate([vfull[hh][j * blk:(j + 2) * blk], ones], axis=1)
            s = _dot_nt(qst, kb)
            s = jnp.where(valid, s, NEG)
            sink = sink_ref[hh]
            m = jnp.maximum(jnp.max(s, axis=-1, keepdims=True), sink)
            e = jnp.exp2(s - jnp.concatenate([m, m], axis=1))
            r = _dot(e.astype(BF16), vb)
            den = r[:, LANES:] + jnp.exp2(sink - m)
            o = r[:, :LANES] * (1.0 / den)
            for a in range(ATTN_GROUP // 2):
                slab = jnp.where(q_lo, o[(2 * a) * blk:(2 * a + 1) * blk],
                                 o[(2 * a + 1) * blk:(2 * a + 2) * blk])
                c0 = (hh * ATTN_GROUP // 2 + a) * LANES
                attn_ref[j * blk:(j + 1) * blk, c0:c0 + LANES] = slab.astype(BF16)

    pe = jnp.concatenate([pcarry_ref[...], p], axis=0)
    pcarry_ref[...] = p[tile - POOL_CARRY:]
    cnt = (t * tile + 1 + lax.broadcasted_iota(jnp.int32, (tile, POOL_GROUP_DIM), 0)).astype(F32)
    pooled = []
    for gi, w in enumerate(POOL_WINDOWS):
        u = pe[:, gi * POOL_GROUP_DIM:(gi + 1) * POOL_GROUP_DIM]
        sm = u
        sh = 1
        while sh < w:
            sm = sm + pltpu.roll(sm, sh, 0)
            sh *= 2
        mean = sm[POOL_CARRY:] / jnp.minimum(cnt, float(w))
        d = (mean - u[POOL_CARRY:]).astype(BF16)
        yg = _dot(d, pw_ref[gi]) * ps_ref[:, gi * POOL_GROUP_DIM:(gi + 1) * POOL_GROUP_DIM]
        pooled.append(yg.astype(BF16))
    mixed = jnp.concatenate(pooled, axis=1)
    o_ref[0] = (x + _dot(attn_ref[...], wout_ref[:ATTN_WIDTH, :])
                + _dot(mixed, wout_ref[ATTN_WIDTH:, :]))


def _even_mixer(x, nw, w_in, sink_rep, pw, ps, w_out, layer, *, tile, casts=()):
    b, s, d = x.shape
    return _fused_call(
        functools.partial(_even_kernel, tile=tile),
        name="even_mixer",
        grid=(b, s // tile),
        in_specs=[
            pl.BlockSpec((1, tile, d), lambda i, j: (i, j, 0)),
            _const_spec(nw.shape),
            _weight_spec(w_in, layer),
            _const_spec(sink_rep.shape),
            _const_spec(pw.shape),
            _const_spec(ps.shape),
            _weight_spec(w_out, layer),
        ],
        operands=(x, nw, w_in, sink_rep, pw, ps, w_out),
        out_spec=pl.BlockSpec((1, tile, d), lambda i, j: (i, j, 0)),
        out_shape=jax.ShapeDtypeStruct((b, s, d), F32),
        scratch_shapes=[
            pltpu.VMEM((ATTN_KV_HEADS, ATTN_BLOCK, KV_WIDTH), BF16),
            pltpu.VMEM((ATTN_KV_HEADS, ATTN_BLOCK, KV_WIDTH), BF16),
            pltpu.VMEM((POOL_CARRY, POOL_WIDTH), F32),
            pltpu.VMEM((tile, ATTN_WIDTH), BF16),
        ],
        casts=casts,
    )


def _softplus(x):
    return jnp.maximum(x, 0.0) + jnp.log(1.0 + jnp.exp(-jnp.abs(x)))


def _odd_kernel(x_ref, nw_ref, win_ref, cw_ref, cb_ref, dtb_ref, alog_ref, dexp_ref, gnw_ref, wout_ref, o_ref,
                state_ref, ccarry_ref, act_ref, z_ref, y_ref, *, tile, inner, heads):
    t = pl.program_id(1)
    L = SSD_CHUNK
    nchunk = tile // L
    gn = SSD_GROUPS * SSD_STATE
    conv_dim = inner + 2 * gn
    heads_per_group = heads // SSD_GROUPS
    pairs_per_group = heads_per_group // 2
    gw = heads_per_group * SSD_HEAD_DIM

    @pl.when(t == 0)
    def _():
        state_ref[...] = jnp.zeros_like(state_ref)
        ccarry_ref[...] = jnp.zeros_like(ccarry_ref)

    x = x_ref[0]
    hn = _rmsnorm(x, nw_ref[...]).astype(BF16)

    def conv_cols(c0, width):
        xb = _dot(hn, win_ref[:, inner + c0:inner + c0 + width])
        xe = jnp.concatenate([ccarry_ref[:, c0:c0 + width], xb], axis=0)
        ccarry_ref[:, c0:c0 + width] = xb[tile - CONV_CARRY:]
        x1 = pltpu.roll(xe, 1, 0)
        w0, w1, w2, w3 = (cw_ref[kk:kk + 1, c0:c0 + width] for kk in range(SSD_CONV))
        acc = (xe * w3 + x1 * w2) + pltpu.roll(xe * w1 + x1 * w0, 2, 0)
        h = acc[CONV_CARRY:] + cb_ref[:, c0:c0 + width]
        act_ref[:, c0:c0 + width] = h + h * jnp.tanh(h)

    def z_cols(c0, width):
        z_ref[:, c0:c0 + width] = _dot(hn, win_ref[:, c0:c0 + width])

    def project_group_items(g):
        items = []
        for c0 in range(g * gw, (g + 1) * gw, PIECE_COLS):
            items.append(functools.partial(conv_cols, c0, PIECE_COLS))
            items.append(functools.partial(z_cols, c0, PIECE_COLS))
        return items

    def finish_group_items(g):
        g0 = g * gw
        cell = []

        def gate():
            yg = y_ref[:, g0:g0 + gw] + act_ref[:, g0:g0 + gw] * dexp_ref[:, g0:g0 + gw]
            yg = yg * _silu(z_ref[:, g0:g0 + gw])
            ms = jnp.mean(yg * yg, axis=-1, keepdims=True)
            cell.append((yg * lax.rsqrt(ms + EPS) * gnw_ref[:, g0:g0 + gw]).astype(BF16))

        def out_proj():
            part = _dot(cell.pop(), wout_ref[g0:g0 + gw, :])
            if g == 0:
                o_ref[0] = x + part
            else:
                o_ref[0] = o_ref[0] + part

        return [gate, out_proj]

    fill = [functools.partial(conv_cols, c0, PIECE_COLS) for c0 in range(inner, conv_dim, PIECE_COLS)]
    fill += project_group_items(0)

    def emit_fill(n):
        for _ in range(n):
            if fill:
                fill.pop(0)()

    dtr = _dot(hn, win_ref[:, inner + conv_dim:])
    emit_fill(1)

    a_neg = -jnp.exp(alog_ref[...])

    li = lax.broadcasted_iota(jnp.int32, (L, L), 0)
    si = lax.broadcasted_iota(jnp.int32, (L, L), 1)
    causal = si <= li
    tril = jnp.where(causal, 1.0, 0.0).astype(BF16)
    plane = lax.broadcasted_iota(jnp.int32, (L, LANES), 1)
    p_lo = plane < SSD_HEAD_DIM

    chunks = []
    for c in range(nchunk):
        r0 = c * L
        dt = _softplus(dtr[r0:r0 + L] + dtb_ref[...])
        a_cs = sum(_dot(tril, part) for part in _split3(dt * a_neg))
        emit_fill(1)
        a2 = a_cs * LOG2E
        w = dt * jnp.exp(a_cs[L - 1:L, :] - a_cs)
        shift = a2 - jnp.log2(dt)
        chunks.append(dict(a2=a2, wT=w.T[:heads], a2_shiftT=shift.T[:heads]))
        emit_fill(1)
    emit_fill(len(fill))

    pending = []
    pairs_per_step = nchunk * pairs_per_group
    for g in range(SSD_GROUPS):
        if g + 1 < SSD_GROUPS:
            pending.extend(project_group_items(g + 1))
        n_items, n_pairs = len(pending), 0
        g0 = g * gw
        for c, ck in enumerate(chunks):
            r0 = c * L
            bg = act_ref[r0:r0 + L, inner + g * SSD_STATE:inner + (g + 1) * SSD_STATE]
            cg = act_ref[r0:r0 + L, inner + gn + g * SSD_STATE:inner + gn + (g + 1) * SSD_STATE].astype(BF16)
            bgT = bg.T.astype(BF16)
            cb = _dot(cg, bgT).astype(BF16)
            y_off = _dot(cg, state_ref[:, g0:g0 + gw].astype(BF16))
            for pr in range(pairs_per_group):
                h0 = g * heads_per_group + 2 * pr
                c0 = h0 * SSD_HEAD_DIM
                xp = act_ref[r0:r0 + L, c0:c0 + LANES]
                w_a = jnp.concatenate([jnp.where(p_lo, xp, 0.0), jnp.where(p_lo, 0.0, xp)], axis=0).astype(BF16)
                m_parts, b_parts, e_parts = [], [], []
                for hd in (h0, h0 + 1):
                    acol = jnp.broadcast_to(ck["a2"][:, hd:hd + 1], (L, L))
                    dec = jnp.exp2(jnp.where(causal, acol - ck["a2_shiftT"][hd:hd + 1, :], NEG))
                    m_parts.append(cb * dec.astype(BF16))
                    b_parts.append(bgT * ck["wT"][hd:hd + 1, :].astype(BF16))
                    e_parts.append(jnp.exp2(acol))
                lhs = jnp.concatenate([jnp.concatenate(m_parts, axis=1),
                                       jnp.concatenate(b_parts, axis=1)], axis=0)
                ra = _dot(lhs, w_a)
                ea_pair = jnp.where(p_lo, e_parts[0], e_parts[1])
                y_ref[r0:r0 + L, c0:c0 + LANES] = ra[:L] + y_off[:, c0 - g0:c0 - g0 + LANES] * ea_pair
                state_ref[:, c0:c0 + LANES] = state_ref[:, c0:c0 + LANES] * ea_pair[L - 1:L, :] + ra[L:]
                n_pairs += 1
                while len(pending) > n_items - n_pairs * n_items // pairs_per_step:
                    pending.pop(0)()
        pending.extend(finish_group_items(g))
    while pending:
        pending.pop(0)()


def _odd_mixer(x, nw, w_in, cw, cb, dtb, alog, dexp, gnw, w_out, *, heads, tile, casts=()):
    b, s, d = x.shape
    inner = heads * SSD_HEAD_DIM
    conv_dim = inner + 2 * SSD_GROUPS * SSD_STATE
    operands = (x, nw, w_in, cw, cb, dtb, alog, dexp, gnw, w_out)
    return _fused_call(
        functools.partial(_odd_kernel, tile=tile, inner=inner, heads=heads),
        name="odd_mixer",
        grid=(b, s // tile),
        in_specs=[pl.BlockSpec((1, tile, d), lambda i, j: (i, j, 0))]
        + [_const_spec(c.shape) for c in operands[1:]],
        operands=operands,
        out_spec=pl.BlockSpec((1, tile, d), lambda i, j: (i, j, 0)),
        out_shape=jax.ShapeDtypeStruct((b, s, d), F32),
        scratch_shapes=[
            pltpu.VMEM((SSD_STATE, inner), F32),
            pltpu.VMEM((CONV_CARRY, conv_dim), F32),
            pltpu.VMEM((tile, conv_dim), F32),
            pltpu.VMEM((tile, inner), F32),
            pltpu.VMEM((tile, inner), F32),
        ],
        casts=casts,
    )


def _row(v):
    return v.reshape(1, -1).astype(F32)


def kernel(x, mix_norm_w, ap_w_in, ap_sinks, pool_w, pool_scale, ap_w_out, ssd_w_in, ssd_conv_w, ssd_conv_b, ssd_dt_bias, ssd_A_log, ssd_D, ssd_norm_w, ssd_w_out, ffn_norm_w, w_gate, w_up, w_down, final_norm_w):
    b, s, d = x.shape
    depth = mix_norm_w.shape[0]
    heads = ssd_dt_bias.shape[1]
    inner = heads * SSD_HEAD_DIM
    conv_dim = inner + 2 * SSD_GROUPS * SSD_STATE
    even_tile = min(EVEN_TILE, s)
    odd_tile = min(ODD_TILE, s)
    ffn_tile = min(FFN_TILE, b * s)

    ap_w_in_b, ap_w_out_b = _to_bf16(ap_w_in), _to_bf16(ap_w_out)
    ssd_w_in_t = jnp.swapaxes(ssd_w_in, 1, 2)

    def ffn_casts(layer):
        return (_Cast(w_gate, layer), _Cast(w_up, layer), _Cast(w_down, layer))

    def ssd_casts(i):
        return (_Cast(ssd_w_in_t, i, transpose=True), _Cast(ssd_w_out, i))

    def lane_pad(a):
        return jnp.pad(a, ((0, 0), (0, LANES - heads)))

    for layer in range(depth):
        i = layer // 2
        nw = _row(mix_norm_w[layer])
        if layer % 2 == 0:
            sink_rep = jnp.broadcast_to(
                (ap_sinks[i].astype(F32) * LOG2E).reshape(ATTN_KV_HEADS, ATTN_GROUP, 1, 1),
                (ATTN_KV_HEADS, ATTN_GROUP, ATTN_BLOCK, LANES)).reshape(ATTN_KV_HEADS, ATTN_GROUP * ATTN_BLOCK, LANES)
            carried = ffn_casts(layer) + (ssd_casts(i) if layer + 1 < depth else ())
            x, ready = _even_mixer(x, nw, ap_w_in_b, sink_rep, pool_w[i].astype(BF16), _row(pool_scale[i]),
                                   ap_w_out_b, i, tile=even_tile, casts=carried)
            ffn_w, ssd_w = ready[:3], ready[3:]
        else:
            x, ffn_w = _odd_mixer(x, nw, ssd_w[0], 0.5 * ssd_conv_w[i].astype(F32), 0.5 * _row(ssd_conv_b[i]),
                                  lane_pad(_row(ssd_dt_bias[i])), lane_pad(_row(ssd_A_log[i])),
                                  _row(jnp.repeat(ssd_D[i], SSD_HEAD_DIM)), _row(ssd_norm_w[i]),
                                  ssd_w[1], heads=heads, tile=odd_tile, casts=ffn_casts(layer))
        x = _ffn(x.reshape(b * s, d), _row(ffn_norm_w[layer]), *ffn_w, _row(final_norm_w),
                 final=layer == depth - 1, tile=ffn_tile).reshape(b, s, d)
    return x
```

```python
import functools
from typing import NamedTuple

import jax
import jax.numpy as jnp
from jax import lax
from jax.experimental import pallas as pl
from jax.experimental.pallas import tpu as pltpu

F32 = jnp.float32
BF16 = jnp.bfloat16

EPS = 1e-5
LANES = 128
BF16_SUBLANES = 16
PIECE_COLS = 256
ATTN_HEADS = 8
ATTN_KV_HEADS = 2
HEAD_DIM = 64
ATTN_BLOCK = 128
ATTN_GROUP = ATTN_HEADS // ATTN_KV_HEADS
ATTN_WIDTH = ATTN_HEADS * HEAD_DIM
KV_WIDTH = ATTN_KV_HEADS * HEAD_DIM
POOL_WINDOWS = (2, 4, 8, 16)
POOL_GROUP_DIM = 128
POOL_WIDTH = len(POOL_WINDOWS) * POOL_GROUP_DIM
POOL_CARRY = 16
SSD_HEAD_DIM = 64
SSD_GROUPS = 4
SSD_STATE = 128
SSD_CONV = 4
SSD_CHUNK = 128
CONV_CARRY = 8
assert SSD_CONV == 4
NEG = -1e30
LOG2E = 1.4426950408889634

VMEM_LIMIT = 56 * 1024 * 1024
CAST_BLOCK_BYTES = 6 * 1024 * 1024

EVEN_TILE = 1024
ODD_TILE = 512
FFN_TILE = 1024
FFN_HIDDEN_CHUNK = 256


def _dot(a, b):
    return jnp.dot(a, b, preferred_element_type=F32)


def _dot_nt(a, b):
    return lax.dot_general(a, b, (((1,), (1,)), ((), ())), preferred_element_type=F32)


def _rmsnorm(x, w):
    ms = jnp.mean(x * x, axis=-1, keepdims=True)
    return x * lax.rsqrt(ms + EPS) * w


def _silu(x):
    h = 0.5 * x
    return h + h * jnp.tanh(h)


def _split3(v):
    hi = v.astype(BF16)
    r = v - hi.astype(F32)
    mid = r.astype(BF16)
    lo = (r - mid.astype(F32)).astype(BF16)
    return hi, mid, lo


def _const_spec(shape):
    nd = len(shape)
    return pl.BlockSpec(shape, lambda *_: (0,) * nd, pipeline_mode=pl.Buffered(1))


def _layer_spec(w, layer):
    return pl.BlockSpec((None,) + w.shape[1:], lambda *_: (layer, 0, 0), pipeline_mode=pl.Buffered(1))


def _cast_kernel(x_ref, o_ref):
    o_ref[...] = x_ref[...].astype(BF16)


def _to_bf16(w):
    nl, r, c = w.shape
    assert c % LANES == 0 and r % BF16_SUBLANES == 0
    rb = r
    while rb * c * w.dtype.itemsize > CAST_BLOCK_BYTES and rb % (2 * BF16_SUBLANES) == 0:
        rb //= 2
    return pl.pallas_call(
        _cast_kernel,
        grid=(nl, r // rb),
        in_specs=[pl.BlockSpec((1, rb, c), lambda i, j: (i, j, 0))],
        out_specs=pl.BlockSpec((1, rb, c), lambda i, j: (i, j, 0)),
        out_shape=jax.ShapeDtypeStruct((nl, r, c), BF16),
        compiler_params=pltpu.CompilerParams(
            dimension_semantics=("arbitrary", "arbitrary"), vmem_limit_bytes=VMEM_LIMIT),
        name="cast_bf16",
    )(w)


class _Cast(NamedTuple):
    w: jax.Array
    layer: int
    transpose: bool = False


def _cast_plan(cast, nsteps):
    if cast.transpose:
        _, c, r = cast.w.shape
        nstrips = max(k for k in range(1, nsteps + 1) if nsteps % k == 0 and (r // LANES) % k == 0)
        c_out = -(-c // LANES) * LANES
        plan = ((None, c, r // nstrips), lambda s: (cast.layer, 0, s), (r // nstrips, c_out), (r, c_out))
    else:
        _, r, c = cast.w.shape
        nstrips = max(k for k in range(1, nsteps + 1)
                      if nsteps % k == 0 and r % k == 0 and (r // k) % BF16_SUBLANES == 0)
        rb = r // nstrips
        plan = ((None, rb, c), lambda s: (cast.layer, s, 0), (rb, c), (r, c))
    assert nsteps % nstrips == 0
    return plan + (nsteps // nstrips,)


def _emit_cast(transpose, reps, step, i_ref, o_ref):
    if not transpose:
        o_ref[...] = i_ref[...].astype(BF16)
        return

    @pl.when(step % reps == 0)
    def _():
        c, width = i_ref.shape
        aligned = c // LANES * LANES
        xt = i_ref[...]
        parts = [xt[:aligned].T]
        if aligned < c:
            tail = jnp.concatenate([xt[aligned:], jnp.zeros((LANES - (c - aligned), width), F32)], axis=0)
            parts.append(tail.T)
        o_ref[...] = jnp.concatenate(parts, axis=1).astype(BF16)


def _fused_kernel(*refs, body, n_in, cast_meta, grid):
    n_c = len(cast_meta)
    ins, cast_ins = refs[:n_in], refs[n_in:n_in + n_c]
    out, cast_outs = refs[n_in + n_c], refs[n_in + n_c + 1:n_in + 2 * n_c + 1]
    scratch = refs[n_in + 2 * n_c + 1:]
    body(*ins, out, *scratch)
    step = pl.program_id(0)
    for axis in range(1, len(grid)):
        step = step * grid[axis] + pl.program_id(axis)
    for (transpose, reps), i_ref, o_ref in zip(cast_meta, cast_ins, cast_outs):
        _emit_cast(transpose, reps, step, i_ref, o_ref)


def _fused_call(body, *, name, grid, in_specs, operands, out_spec, out_shape, scratch_shapes=(), casts=()):
    nsteps = 1
    for g in grid:
        nsteps *= g

    def flat(idx):
        s = idx[0]
        for axis in range(1, len(grid)):
            s = s * grid[axis] + idx[axis]
        return s

    cast_in_specs, cast_out_specs, cast_out_shapes, cast_meta = [], [], [], []
    for cast in casts:
        in_block, in_index, out_block, o_shape, reps = _cast_plan(cast, nsteps)
        cast_in_specs.append(pl.BlockSpec(in_block, lambda *g, f=in_index, r=reps: f(flat(g) // r)))
        cast_out_specs.append(pl.BlockSpec(out_block, lambda *g, r=reps: (flat(g) // r, 0)))
        cast_out_shapes.append(jax.ShapeDtypeStruct(o_shape, BF16))
        cast_meta.append((cast.transpose, reps))
    outs = pl.pallas_call(
        functools.partial(_fused_kernel, body=body, n_in=len(in_specs), cast_meta=tuple(cast_meta), grid=grid),
        grid=grid,
        in_specs=list(in_specs) + cast_in_specs,
        out_specs=[out_spec] + cast_out_specs,
        out_shape=[out_shape] + cast_out_shapes,
        scratch_shapes=list(scratch_shapes),
        compiler_params=pltpu.CompilerParams(
            dimension_semantics=("arbitrary",) * len(grid), vmem_limit_bytes=VMEM_LIMIT),
        name=name,
    )(*operands, *[c.w for c in casts])
    return outs[0], list(outs[1:])


def _weight_spec(w, layer):
    return _const_spec(w.shape) if w.ndim == 2 else _layer_spec(w, layer)


def _ffn_kernel(x_ref, nw_ref, wg_ref, wu_ref, wd_ref, fw_ref, o_ref, *, final):
    x = x_ref[...]
    h = _rmsnorm(x, nw_ref[...]).astype(BF16)
    y = x
    for c0 in range(0, wg_ref.shape[1], FFN_HIDDEN_CHUNK):
        cols = slice(c0, c0 + FFN_HIDDEN_CHUNK)
        g = _dot(h, wg_ref[:, cols])
        u = _dot(h, wu_ref[:, cols])
        a = (_silu(g) * u).astype(BF16)
        y = y + _dot(a, wd_ref[cols, :])
    if final:
        y = _rmsnorm(y, fw_ref[...])
    o_ref[...] = y


def _ffn(x2d, nw, wg, wu, wd, fw, *, final, tile):
    n, d = x2d.shape
    assert n % tile == 0 and wg.shape[1] % FFN_HIDDEN_CHUNK == 0
    return _fused_call(
        functools.partial(_ffn_kernel, final=final),
        name="ffn",
        grid=(n // tile,),
        in_specs=[
            pl.BlockSpec((tile, d), lambda i: (i, 0)),
            _const_spec((1, d)),
            _const_spec(wg.shape),
            _const_spec(wu.shape),
            _const_spec(wd.shape),
            _const_spec((1, d)),
        ],
        operands=(x2d, nw, wg, wu, wd, fw),
        out_spec=pl.BlockSpec((tile, d), lambda i: (i, 0)),
        out_shape=jax.ShapeDtypeStruct((n, d), F32),
    )[0]


def _even_kernel(x_ref, nw_ref, win_ref, sink_ref, pw_ref, ps_ref, wout_ref, o_ref,
                 kprev_ref, vprev_ref, pcarry_ref, attn_ref, *, tile):
    t = pl.program_id(1)
    nblk = tile // ATTN_BLOCK
    blk = ATTN_BLOCK

    @pl.when(t == 0)
    def _():
        kprev_ref[...] = jnp.zeros_like(kprev_ref)
        vprev_ref[...] = jnp.zeros_like(vprev_ref)
        pcarry_ref[...] = jnp.zeros_like(pcarry_ref)

    x = x_ref[0]
    h = _rmsnorm(x, nw_ref[...]).astype(BF16)
    proj = _dot(h, win_ref[...])
    q = proj[:, :ATTN_WIDTH] * (HEAD_DIM ** -0.5 * LOG2E)
    k = proj[:, ATTN_WIDTH:ATTN_WIDTH + KV_WIDTH]
    v = proj[:, ATTN_WIDTH + KV_WIDTH:ATTN_WIDTH + 2 * KV_WIDTH]
    p = proj[:, ATTN_WIDTH + 2 * KV_WIDTH:]
    q_is_even = lax.broadcasted_iota(jnp.int32, q.shape, 1) % LANES < HEAD_DIM
    q_par = (jnp.where(q_is_even, q, 0.0).astype(BF16), jnp.where(q_is_even, 0.0, q).astype(BF16))

    lane = lax.broadcasted_iota(jnp.int32, (tile, KV_WIDTH), 1)
    lo_half = lane < HEAD_DIM

    def both_halves(a):
        sw = pltpu.roll(a, HEAD_DIM, 1)
        return (jnp.where(lo_half, a, sw).astype(BF16), jnp.where(lo_half, sw, a).astype(BF16))

    k2 = both_halves(k)
    v2 = both_halves(v)
    kfull = [jnp.concatenate([kprev_ref[hh], k2[hh]], axis=0) for hh in range(ATTN_KV_HEADS)]
    vfull = [jnp.concatenate([vprev_ref[hh], v2[hh]], axis=0) for hh in range(ATTN_KV_HEADS)]
    for hh in range(ATTN_KV_HEADS):
        kprev_ref[hh] = k2[hh][tile - blk:]
        vprev_ref[hh] = v2[hh][tile - blk:]

    rows = ATTN_GROUP * blk
    qi = lax.broadcasted_iota(jnp.int32, (rows, 2 * blk), 0) % blk
    kj = lax.broadcasted_iota(jnp.int32, (rows, 2 * blk), 1)
    band = (kj > qi) & (kj <= qi + blk)
    first_key = jnp.where(t == 0, blk, 0)
    qlane = lax.broadcasted_iota(jnp.int32, (blk, LANES), 1)
    q_lo = qlane < HEAD_DIM
    ones = jnp.ones((2 * blk, LANES), BF16)

    for j in range(nblk):
        valid = band & (kj >= first_key) if j == 0 else band
        for hh in range(ATTN_KV_HEADS):
            parts = []
            for g in range(ATTN_GROUP):
                c0 = (hh * ATTN_GROUP + g) // 2 * LANES
                parts.append(q_par[g % 2][j * blk:(j + 1) * blk, c0:c0 + LANES])
            qst = jnp.concatenate(parts, axis=0)
            kb = kfull[hh][j * blk:(j + 2) * blk]
            vb = jnp.concatenate([vfull[hh][j * blk:(j + 2) * blk], ones], axis=1)
            s = _dot_nt(qst, kb)
            s = jnp.where(valid, s, NEG)
            sink = sink_ref[hh]
            m = jnp.maximum(jnp.max(s, axis=-1, keepdims=True), sink)
            e = jnp.exp2(s - jnp.concatenate([m, m], axis=1))
            r = _dot(e.astype(BF16), vb)
            den = r[:, LANES:] + jnp.exp2(sink - m)
            o = r[:, :LANES] * (1.0 / den)
            for a in range(ATTN_GROUP // 2):
                slab = jnp.where(q_lo, o[(2 * a) * blk:(2 * a + 1) * blk],
                                 o[(2 * a + 1) * blk:(2 * a + 2) * blk])
                c0 = (hh * ATTN_GROUP // 2 + a) * LANES
                attn_ref[j * blk:(j + 1) * blk, c0:c0 + LANES] = slab.astype(BF16)

    pe = jnp.concatenate([pcarry_ref[...], p], axis=0)
    pcarry_ref[...] = p[tile - POOL_CARRY:]
    cnt = (t * tile + 1 + lax.broadcasted_iota(jnp.int32, (tile, POOL_GROUP_DIM), 0)).astype(F32)
    pooled = []
    for gi, w in enumerate(POOL_WINDOWS):
        u = pe[:, gi * POOL_GROUP_DIM:(gi + 1) * POOL_GROUP_DIM]
        sm = u
        sh = 1
        while sh < w:
            sm = sm + pltpu.roll(sm, sh, 0)
            sh *= 2
        mean = sm[POOL_CARRY:] / jnp.minimum(cnt, float(w))
        d = (mean - u[POOL_CARRY:]).astype(BF16)
        yg = _dot(d, pw_ref[gi]) * ps_ref[:, gi * POOL_GROUP_DIM:(gi + 1) * POOL_GROUP_DIM]
        pooled.append(yg.astype(BF16))
    mixed = jnp.concatenate(pooled, axis=1)
    o_ref[0] = (x + _dot(attn_ref[...], wout_ref[:ATTN_WIDTH, :])
                + _dot(mixed, wout_ref[ATTN_WIDTH:, :]))


def _even_mixer(x, nw, w_in, sink_rep, pw, ps, w_out, layer, *, tile, casts=()):
    b, s, d = x.shape
    assert s % tile == 0 and tile % ATTN_BLOCK == 0
    assert w_in.shape[-1] == ATTN_WIDTH + 2 * KV_WIDTH + POOL_WIDTH and w_out.shape[-2] == ATTN_WIDTH + POOL_WIDTH
    return _fused_call(
        functools.partial(_even_kernel, tile=tile),
        name="even_mixer",
        grid=(b, s // tile),
        in_specs=[
            pl.BlockSpec((1, tile, d), lambda i, j: (i, j, 0)),
            _const_spec(nw.shape),
            _weight_spec(w_in, layer),
            _const_spec(sink_rep.shape),
            _const_spec(pw.shape),
            _const_spec(ps.shape),
            _weight_spec(w_out, layer),
        ],
        operands=(x, nw, w_in, sink_rep, pw, ps, w_out),
        out_spec=pl.BlockSpec((1, tile, d), lambda i, j: (i, j, 0)),
        out_shape=jax.ShapeDtypeStruct((b, s, d), F32),
        scratch_shapes=[
            pltpu.VMEM((ATTN_KV_HEADS, ATTN_BLOCK, KV_WIDTH), BF16),
            pltpu.VMEM((ATTN_KV_HEADS, ATTN_BLOCK, KV_WIDTH), BF16),
            pltpu.VMEM((POOL_CARRY, POOL_WIDTH), F32),
            pltpu.VMEM((tile, ATTN_WIDTH), BF16),
        ],
        casts=casts,
    )


def _softplus(x):
    return jnp.maximum(x, 0.0) + jnp.log(1.0 + jnp.exp(-jnp.abs(x)))


def _odd_kernel(x_ref, nw_ref, win_ref, cw_ref, cb_ref, dtb_ref, alog_ref, dexp_ref, gnw_ref, wout_ref, o_ref,
                state_ref, ccarry_ref, act_ref, z_ref, y_ref, *, tile, inner, heads):
    t = pl.program_id(1)
    L = SSD_CHUNK
    nchunk = tile // L
    gn = SSD_GROUPS * SSD_STATE
    conv_dim = inner + 2 * gn
    heads_per_group = heads // SSD_GROUPS
    pairs_per_group = heads_per_group // 2
    gw = heads_per_group * SSD_HEAD_DIM

    @pl.when(t == 0)
    def _():
        state_ref[...] = jnp.zeros_like(state_ref)
        ccarry_ref[...] = jnp.zeros_like(ccarry_ref)

    x = x_ref[0]
    hn = _rmsnorm(x, nw_ref[...]).astype(BF16)

    def conv_cols(c0, width):
        xb = _dot(hn, win_ref[:, inner + c0:inner + c0 + width])
        xe = jnp.concatenate([ccarry_ref[:, c0:c0 + width], xb], axis=0)
        ccarry_ref[:, c0:c0 + width] = xb[tile - CONV_CARRY:]
        x1 = pltpu.roll(xe, 1, 0)
        w0, w1, w2, w3 = (cw_ref[kk:kk + 1, c0:c0 + width] for kk in range(SSD_CONV))
        acc = (xe * w3 + x1 * w2) + pltpu.roll(xe * w1 + x1 * w0, 2, 0)
        act_ref[:, c0:c0 + width] = _silu(acc[CONV_CARRY:] + cb_ref[:, c0:c0 + width])

    def z_cols(c0, width):
        z_ref[:, c0:c0 + width] = _dot(hn, win_ref[:, c0:c0 + width])

    def project_group_items(g):
        items = []
        for c0 in range(g * gw, (g + 1) * gw, PIECE_COLS):
            items.append(functools.partial(conv_cols, c0, PIECE_COLS))
            items.append(functools.partial(z_cols, c0, PIECE_COLS))
        return items

    def finish_group_items(g):
        g0 = g * gw
        cell = []

        def gate():
            yg = y_ref[:, g0:g0 + gw] + act_ref[:, g0:g0 + gw] * dexp_ref[:, g0:g0 + gw]
            yg = yg * _silu(z_ref[:, g0:g0 + gw])
            ms = jnp.mean(yg * yg, axis=-1, keepdims=True)
            cell.append((yg * lax.rsqrt(ms + EPS) * gnw_ref[:, g0:g0 + gw]).astype(BF16))

        def out_proj():
            part = _dot(cell.pop(), wout_ref[g0:g0 + gw, :])
            if g == 0:
                o_ref[0] = x + part
            else:
                o_ref[0] = o_ref[0] + part

        return [gate, out_proj]

    fill = [functools.partial(conv_cols, c0, PIECE_COLS) for c0 in range(inner, conv_dim, PIECE_COLS)]
    fill += project_group_items(0)

    def emit_fill(n):
        for _ in range(n):
            if fill:
                fill.pop(0)()

    dtr = _dot(hn, win_ref[:, inner + conv_dim:])
    emit_fill(1)

    a_neg = -jnp.exp(alog_ref[...])

    li = lax.broadcasted_iota(jnp.int32, (L, L), 0)
    si = lax.broadcasted_iota(jnp.int32, (L, L), 1)
    causal = si <= li
    tril = jnp.where(causal, 1.0, 0.0).astype(BF16)
    plane = lax.broadcasted_iota(jnp.int32, (L, LANES), 1)
    p_lo = plane < SSD_HEAD_DIM

    chunks = []
    for c in range(nchunk):
        r0 = c * L
        dt = _softplus(dtr[r0:r0 + L] + dtb_ref[...])
        a_cs = sum(_dot(tril, part) for part in _split3(dt * a_neg))
        emit_fill(1)
        a2 = a_cs * LOG2E
        w = dt * jnp.exp(a_cs[L - 1:L, :] - a_cs)
        shift = a2 - jnp.log2(dt)
        chunks.append(dict(a2=a2, wT=w.T[:heads], a2_shiftT=shift.T[:heads]))
        emit_fill(1)
    emit_fill(len(fill))

    pending = []
    pairs_per_step = nchunk * pairs_per_group
    for g in range(SSD_GROUPS):
        if g + 1 < SSD_GROUPS:
            pending.extend(project_group_items(g + 1))
        n_items, n_pairs = len(pending), 0
        g0 = g * gw
        for c, ck in enumerate(chunks):
            r0 = c * L
            bg = act_ref[r0:r0 + L, inner + g * SSD_STATE:inner + (g + 1) * SSD_STATE]
            cg = act_ref[r0:r0 + L, inner + gn + g * SSD_STATE:inner + gn + (g + 1) * SSD_STATE].astype(BF16)
            bgT = bg.T.astype(BF16)
            cb = _dot(cg, bgT).astype(BF16)
            y_off = _dot(cg, state_ref[:, g0:g0 + gw].astype(BF16))
            for pr in range(pairs_per_group):
                h0 = g * heads_per_group + 2 * pr
                c0 = h0 * SSD_HEAD_DIM
                xp = act_ref[r0:r0 + L, c0:c0 + LANES]
                w_a = jnp.concatenate([jnp.where(p_lo, xp, 0.0), jnp.where(p_lo, 0.0, xp)], axis=0).astype(BF16)
                m_parts, b_parts, e_parts = [], [], []
                for hd in (h0, h0 + 1):
                    acol = jnp.broadcast_to(ck["a2"][:, hd:hd + 1], (L, L))
                    dec = jnp.exp2(jnp.where(causal, acol - ck["a2_shiftT"][hd:hd + 1, :], NEG))
                    m_parts.append(cb * dec.astype(BF16))
                    b_parts.append(bgT * ck["wT"][hd:hd + 1, :].astype(BF16))
                    e_parts.append(jnp.exp2(acol))
                lhs = jnp.concatenate([jnp.concatenate(m_parts, axis=1),
                                       jnp.concatenate(b_parts, axis=1)], axis=0)
                ra = _dot(lhs, w_a)
                ea_pair = jnp.where(p_lo, e_parts[0], e_parts[1])
                y_ref[r0:r0 + L, c0:c0 + LANES] = ra[:L] + y_off[:, c0 - g0:c0 - g0 + LANES] * ea_pair
                state_ref[:, c0:c0 + LANES] = state_ref[:, c0:c0 + LANES] * ea_pair[L - 1:L, :] + ra[L:]
                n_pairs += 1
                while len(pending) > n_items - n_pairs * n_items // pairs_per_step:
                    pending.pop(0)()
        pending.extend(finish_group_items(g))
    while pending:
        pending.pop(0)()


def _odd_mixer(x, nw, w_in, cw, cb, dtb, alog, dexp, gnw, w_out, *, heads, tile, casts=()):
    b, s, d = x.shape
    inner = heads * SSD_HEAD_DIM
    conv_dim = inner + 2 * SSD_GROUPS * SSD_STATE
    assert s % tile == 0 and tile % SSD_CHUNK == 0 and heads % (2 * SSD_GROUPS) == 0 and heads <= LANES
    assert inner % PIECE_COLS == 0 and w_in.shape == (d, inner + conv_dim + LANES)
    operands = (x, nw, w_in, cw, cb, dtb, alog, dexp, gnw, w_out)
    return _fused_call(
        functools.partial(_odd_kernel, tile=tile, inner=inner, heads=heads),
        name="odd_mixer",
        grid=(b, s // tile),
        in_specs=[pl.BlockSpec((1, tile, d), lambda i, j: (i, j, 0))]
        + [_const_spec(c.shape) for c in operands[1:]],
        operands=operands,
        out_spec=pl.BlockSpec((1, tile, d), lambda i, j: (i, j, 0)),
        out_shape=jax.ShapeDtypeStruct((b, s, d), F32),
        scratch_shapes=[
            pltpu.VMEM((SSD_STATE, inner), F32),
            pltpu.VMEM((CONV_CARRY, conv_dim), F32),
            pltpu.VMEM((tile, conv_dim), F32),
            pltpu.VMEM((tile, inner), F32),
            pltpu.VMEM((tile, inner), F32),
        ],
        casts=casts,
    )


def _row(v):
    return v.reshape(1, -1).astype(F32)


def kernel(x, mix_norm_w, ap_w_in, ap_sinks, pool_w, pool_scale, ap_w_out, ssd_w_in, ssd_conv_w, ssd_conv_b, ssd_dt_bias, ssd_A_log, ssd_D, ssd_norm_w, ssd_w_out, ffn_norm_w, w_gate, w_up, w_down, final_norm_w):
    b, s, d = x.shape
    depth = mix_norm_w.shape[0]
    heads = ssd_dt_bias.shape[1]
    even_tile = min(EVEN_TILE, s)
    odd_tile = min(ODD_TILE, s)
    ffn_tile = min(FFN_TILE, b * s)

    ap_w_in_b, ap_w_out_b = _to_bf16(ap_w_in), _to_bf16(ap_w_out)
    ssd_w_in_t = jnp.swapaxes(ssd_w_in, 1, 2)

    def ffn_casts(layer):
        return (_Cast(w_gate, layer), _Cast(w_up, layer), _Cast(w_down, layer))

    def ssd_casts(i):
        return (_Cast(ssd_w_in_t, i, transpose=True), _Cast(ssd_w_out, i))

    def lane_pad(a):
        return jnp.pad(a, ((0, 0), (0, LANES - heads)))

    for layer in range(depth):
        i = layer // 2
        nw = _row(mix_norm_w[layer])
        if layer % 2 == 0:
            sink_rep = jnp.broadcast_to(
                (ap_sinks[i].astype(F32) * LOG2E).reshape(ATTN_KV_HEADS, ATTN_GROUP, 1, 1),
                (ATTN_KV_HEADS, ATTN_GROUP, ATTN_BLOCK, LANES)).reshape(ATTN_KV_HEADS, ATTN_GROUP * ATTN_BLOCK, LANES)
            carried = ffn_casts(layer) + (ssd_casts(i) if layer + 1 < depth else ())
            x, ready = _even_mixer(x, nw, ap_w_in_b, sink_rep, pool_w[i].astype(BF16), _row(pool_scale[i]),
                                   ap_w_out_b, i, tile=even_tile, casts=carried)
            ffn_w, ssd_w = ready[:3], ready[3:]
        else:
            x, ffn_w = _odd_mixer(x, nw, ssd_w[0], ssd_conv_w[i].astype(F32), _row(ssd_conv_b[i]),
                                  lane_pad(_row(ssd_dt_bias[i])), lane_pad(_row(ssd_A_log[i])),
                                  _row(jnp.repeat(ssd_D[i], SSD_HEAD_DIM)), _row(ssd_norm_w[i]),
                                  ssd_w[1], heads=heads, tile=odd_tile, casts=ffn_casts(layer))
        x = _ffn(x.reshape(b * s, d), _row(ffn_norm_w[layer]), *ffn_w, _row(final_norm_w),
                 final=layer == depth - 1, tile=ffn_tile).reshape(b, s, d)
    return x
```

```python
import functools
from typing import NamedTuple

import jax
import jax.numpy as jnp
from jax import lax
from jax.experimental import pallas as pl
from jax.experimental.pallas import tpu as pltpu

F32 = jnp.float32
BF16 = jnp.bfloat16

EPS = 1e-5
LANES = 128
BF16_SUBLANES = 16
PIECE_COLS = 256
ATTN_HEADS = 8
ATTN_KV_HEADS = 2
HEAD_DIM = 64
ATTN_BLOCK = 128
ATTN_GROUP = ATTN_HEADS // ATTN_KV_HEADS
ATTN_WIDTH = ATTN_HEADS * HEAD_DIM
KV_WIDTH = ATTN_KV_HEADS * HEAD_DIM
POOL_WINDOWS = (2, 4, 8, 16)
POOL_GROUP_DIM = 128
POOL_WIDTH = len(POOL_WINDOWS) * POOL_GROUP_DIM
POOL_CARRY = 16
SSD_HEAD_DIM = 64
SSD_GROUPS = 4
SSD_STATE = 128
SSD_CONV = 4
SSD_CHUNK = 128
CONV_CARRY = 8
assert SSD_CONV == 4
NEG = -1e30
LOG2E = 1.4426950408889634

VMEM_LIMIT = 56 * 1024 * 1024

EVEN_TILE = 1024
ODD_TILE = 512
FFN_TILE = 1024
FFN_HIDDEN_CHUNK = 256


def _dot(a, b):
    return jnp.dot(a, b, preferred_element_type=F32)


def _dot_nt(a, b):
    return lax.dot_general(a, b, (((1,), (1,)), ((), ())), preferred_element_type=F32)


def _rmsnorm(x, w):
    ms = jnp.mean(x * x, axis=-1, keepdims=True)
    return x * lax.rsqrt(ms + EPS) * w


def _silu(x):
    h = 0.5 * x
    return h + h * jnp.tanh(h)


def _split3(v):
    hi = v.astype(BF16)
    r = v - hi.astype(F32)
    mid = r.astype(BF16)
    lo = (r - mid.astype(F32)).astype(BF16)
    return hi, mid, lo


def _const_spec(shape):
    nd = len(shape)
    return pl.BlockSpec(shape, lambda *_: (0,) * nd, pipeline_mode=pl.Buffered(1))


def _layer_spec(w, layer):
    return pl.BlockSpec((None,) + w.shape[1:], lambda *_: (layer, 0, 0), pipeline_mode=pl.Buffered(1))


class _Cast(NamedTuple):
    w: jax.Array
    layer: int
    transpose: bool = False


def _cast_plan(cast, nsteps):
    if cast.transpose:
        _, c, r = cast.w.shape
        nstrips = max(k for k in range(1, nsteps + 1) if nsteps % k == 0 and (r // LANES) % k == 0)
        c_out = -(-c // LANES) * LANES
        plan = ((None, c, r // nstrips), lambda s: (cast.layer, 0, s), (r // nstrips, c_out), (r, c_out))
    else:
        _, r, c = cast.w.shape
        nstrips = max(k for k in range(1, nsteps + 1)
                      if nsteps % k == 0 and r % k == 0 and (r // k) % BF16_SUBLANES == 0)
        rb = r // nstrips
        plan = ((None, rb, c), lambda s: (cast.layer, s, 0), (rb, c), (r, c))
    assert nsteps % nstrips == 0
    return plan + (nsteps // nstrips,)


def _emit_cast(transpose, reps, step, i_ref, o_ref):
    if not transpose:
        o_ref[...] = i_ref[...].astype(BF16)
        return

    @pl.when(step % reps == 0)
    def _():
        c, width = i_ref.shape
        aligned = c // LANES * LANES
        xt = i_ref[...]
        parts = [xt[:aligned].T]
        if aligned < c:
            tail = jnp.concatenate([xt[aligned:], jnp.zeros((LANES - (c - aligned), width), F32)], axis=0)
            parts.append(tail.T)
        o_ref[...] = jnp.concatenate(parts, axis=1).astype(BF16)


def _fused_kernel(*refs, body, n_in, cast_meta, grid):
    n_c = len(cast_meta)
    ins, cast_ins = refs[:n_in], refs[n_in:n_in + n_c]
    out, cast_outs = refs[n_in + n_c], refs[n_in + n_c + 1:n_in + 2 * n_c + 1]
    scratch = refs[n_in + 2 * n_c + 1:]
    body(*ins, out, *scratch)
    step = pl.program_id(0)
    for axis in range(1, len(grid)):
        step = step * grid[axis] + pl.program_id(axis)
    for (transpose, reps), i_ref, o_ref in zip(cast_meta, cast_ins, cast_outs):
        _emit_cast(transpose, reps, step, i_ref, o_ref)


def _fused_call(body, *, name, grid, in_specs, operands, out_spec, out_shape, scratch_shapes=(), casts=()):
    nsteps = 1
    for g in grid:
        nsteps *= g

    def flat(idx):
        s = idx[0]
        for axis in range(1, len(grid)):
            s = s * grid[axis] + idx[axis]
        return s

    cast_in_specs, cast_out_specs, cast_out_shapes, cast_meta = [], [], [], []
    for cast in casts:
        in_block, in_index, out_block, o_shape, reps = _cast_plan(cast, nsteps)
        cast_in_specs.append(pl.BlockSpec(in_block, lambda *g, f=in_index, r=reps: f(flat(g) // r)))
        cast_out_specs.append(pl.BlockSpec(out_block, lambda *g, r=reps: (flat(g) // r, 0)))
        cast_out_shapes.append(jax.ShapeDtypeStruct(o_shape, BF16))
        cast_meta.append((cast.transpose, reps))
    outs = pl.pallas_call(
        functools.partial(_fused_kernel, body=body, n_in=len(in_specs), cast_meta=tuple(cast_meta), grid=grid),
        grid=grid,
        in_specs=list(in_specs) + cast_in_specs,
        out_specs=[out_spec] + cast_out_specs,
        out_shape=[out_shape] + cast_out_shapes,
        scratch_shapes=list(scratch_shapes),
        compiler_params=pltpu.CompilerParams(
            dimension_semantics=("arbitrary",) * len(grid), vmem_limit_bytes=VMEM_LIMIT),
        name=name,
    )(*operands, *[c.w for c in casts])
    return outs[0], list(outs[1:])


def _ffn_kernel(x_ref, nw_ref, wg_ref, wu_ref, wd_ref, fw_ref, o_ref, *, final):
    x = x_ref[...]
    h = _rmsnorm(x, nw_ref[...]).astype(BF16)
    y = x
    for c0 in range(0, wg_ref.shape[1], FFN_HIDDEN_CHUNK):
        cols = slice(c0, c0 + FFN_HIDDEN_CHUNK)
        g = _dot(h, wg_ref[:, cols])
        u = _dot(h, wu_ref[:, cols])
        a = (_silu(g) * u).astype(BF16)
        y = y + _dot(a, wd_ref[cols, :])
    if final:
        y = _rmsnorm(y, fw_ref[...])
    o_ref[...] = y


def _ffn(x2d, nw, wg, wu, wd, fw, *, final, tile):
    n, d = x2d.shape
    assert n % tile == 0 and wg.shape[1] % FFN_HIDDEN_CHUNK == 0
    return _fused_call(
        functools.partial(_ffn_kernel, final=final),
        name="ffn",
        grid=(n // tile,),
        in_specs=[
            pl.BlockSpec((tile, d), lambda i: (i, 0)),
            _const_spec((1, d)),
            _const_spec(wg.shape),
            _const_spec(wu.shape),
            _const_spec(wd.shape),
            _const_spec((1, d)),
        ],
        operands=(x2d, nw, wg, wu, wd, fw),
        out_spec=pl.BlockSpec((tile, d), lambda i: (i, 0)),
        out_shape=jax.ShapeDtypeStruct((n, d), F32),
    )[0]


def _even_kernel(x_ref, nw_ref, win_f32_ref, sink_ref, pw_ref, ps_ref, wout_f32_ref, o_ref,
                 kprev_ref, vprev_ref, pcarry_ref, attn_ref, win_ref, wout_ref, *, tile):
    t = pl.program_id(1)
    nblk = tile // ATTN_BLOCK
    blk = ATTN_BLOCK

    @pl.when((pl.program_id(0) == 0) & (t == 0))
    def _():
        win_ref[...] = win_f32_ref[...].astype(BF16)
        wout_ref[...] = wout_f32_ref[...].astype(BF16)

    @pl.when(t == 0)
    def _():
        kprev_ref[...] = jnp.zeros_like(kprev_ref)
        vprev_ref[...] = jnp.zeros_like(vprev_ref)
        pcarry_ref[...] = jnp.zeros_like(pcarry_ref)

    x = x_ref[0]
    h = _rmsnorm(x, nw_ref[...]).astype(BF16)
    proj = _dot(h, win_ref[...])
    q = proj[:, :ATTN_WIDTH] * (HEAD_DIM ** -0.5 * LOG2E)
    k = proj[:, ATTN_WIDTH:ATTN_WIDTH + KV_WIDTH]
    v = proj[:, ATTN_WIDTH + KV_WIDTH:ATTN_WIDTH + 2 * KV_WIDTH]
    p = proj[:, ATTN_WIDTH + 2 * KV_WIDTH:]
    q_is_even = lax.broadcasted_iota(jnp.int32, q.shape, 1) % LANES < HEAD_DIM
    q_par = (jnp.where(q_is_even, q, 0.0).astype(BF16), jnp.where(q_is_even, 0.0, q).astype(BF16))

    lane = lax.broadcasted_iota(jnp.int32, (tile, KV_WIDTH), 1)
    lo_half = lane < HEAD_DIM

    def both_halves(a):
        sw = pltpu.roll(a, HEAD_DIM, 1)
        return (jnp.where(lo_half, a, sw).astype(BF16), jnp.where(lo_half, sw, a).astype(BF16))

    k2 = both_halves(k)
    v2 = both_halves(v)
    kfull = [jnp.concatenate([kprev_ref[hh], k2[hh]], axis=0) for hh in range(ATTN_KV_HEADS)]
    vfull = [jnp.concatenate([vprev_ref[hh], v2[hh]], axis=0) for hh in range(ATTN_KV_HEADS)]
    for hh in range(ATTN_KV_HEADS):
        kprev_ref[hh] = k2[hh][tile - blk:]
        vprev_ref[hh] = v2[hh][tile - blk:]

    rows = ATTN_GROUP * blk
    qi = lax.broadcasted_iota(jnp.int32, (rows, 2 * blk), 0) % blk
    kj = lax.broadcasted_iota(jnp.int32, (rows, 2 * blk), 1)
    band = (kj > qi) & (kj <= qi + blk)
    first_key = jnp.where(t == 0, blk, 0)
    qlane = lax.broadcasted_iota(jnp.int32, (blk, LANES), 1)
    q_lo = qlane < HEAD_DIM
    ones = jnp.ones((2 * blk, LANES), BF16)

    for j in range(nblk):
        valid = band & (kj >= first_key) if j == 0 else band
        for hh in range(ATTN_KV_HEADS):
            parts = []
            for g in range(ATTN_GROUP):
                c0 = (hh * ATTN_GROUP + g) // 2 * LANES
                parts.append(q_par[g % 2][j * blk:(j + 1) * blk, c0:c0 + LANES])
            qst = jnp.concatenate(parts, axis=0)
            kb = kfull[hh][j * blk:(j + 2) * blk]
            vb = jnp.concatenate([vfull[hh][j * blk:(j + 2) * blk], ones], axis=1)
            s = _dot_nt(qst, kb)
            s = jnp.where(valid, s, NEG)
            sink = sink_ref[hh]
            m = jnp.maximum(jnp.max(s, axis=-1, keepdims=True), sink)
            e = jnp.exp2(s - jnp.concatenate([m, m], axis=1))
            r = _dot(e.astype(BF16), vb)
            den = r[:, LANES:] + jnp.exp2(sink - m)
            o = r[:, :LANES] * (1.0 / den)
            for a in range(ATTN_GROUP // 2):
                slab = jnp.where(q_lo, o[(2 * a) * blk:(2 * a + 1) * blk],
                                 o[(2 * a + 1) * blk:(2 * a + 2) * blk])
                c0 = (hh * ATTN_GROUP // 2 + a) * LANES
                attn_ref[j * blk:(j + 1) * blk, c0:c0 + LANES] = slab.astype(BF16)

    pe = jnp.concatenate([pcarry_ref[...], p], axis=0)
    pcarry_ref[...] = p[tile - POOL_CARRY:]
    cnt = (t * tile + 1 + lax.broadcasted_iota(jnp.int32, (tile, POOL_GROUP_DIM), 0)).astype(F32)
    pooled = []
    for gi, w in enumerate(POOL_WINDOWS):
        u = pe[:, gi * POOL_GROUP_DIM:(gi + 1) * POOL_GROUP_DIM]
        sm = u
        sh = 1
        while sh < w:
            sm = sm + pltpu.roll(sm, sh, 0)
            sh *= 2
        mean = sm[POOL_CARRY:] / jnp.minimum(cnt, float(w))
        d = (mean - u[POOL_CARRY:]).astype(BF16)
        yg = _dot(d, pw_ref[gi]) * ps_ref[:, gi * POOL_GROUP_DIM:(gi + 1) * POOL_GROUP_DIM]
        pooled.append(yg.astype(BF16))
    mixed = jnp.concatenate(pooled, axis=1)
    o_ref[0] = (x + _dot(attn_ref[...], wout_ref[:ATTN_WIDTH, :])
                + _dot(mixed, wout_ref[ATTN_WIDTH:, :]))


def _even_mixer(x, nw, w_in, sink_rep, pw, ps, w_out, layer, *, tile, casts=()):
    b, s, d = x.shape
    assert s % tile == 0 and tile % ATTN_BLOCK == 0
    assert w_in.shape[-1] == ATTN_WIDTH + 2 * KV_WIDTH + POOL_WIDTH and w_out.shape[-2] == ATTN_WIDTH + POOL_WIDTH
    return _fused_call(
        functools.partial(_even_kernel, tile=tile),
        name="even_mixer",
        grid=(b, s // tile),
        in_specs=[
            pl.BlockSpec((1, tile, d), lambda i, j: (i, j, 0)),
            _const_spec(nw.shape),
            _layer_spec(w_in, layer),
            _const_spec(sink_rep.shape),
            _const_spec(pw.shape),
            _const_spec(ps.shape),
            _layer_spec(w_out, layer),
        ],
        operands=(x, nw, w_in, sink_rep, pw, ps, w_out),
        out_spec=pl.BlockSpec((1, tile, d), lambda i, j: (i, j, 0)),
        out_shape=jax.ShapeDtypeStruct((b, s, d), F32),
        scratch_shapes=[
            pltpu.VMEM((ATTN_KV_HEADS, ATTN_BLOCK, KV_WIDTH), BF16),
            pltpu.VMEM((ATTN_KV_HEADS, ATTN_BLOCK, KV_WIDTH), BF16),
            pltpu.VMEM((POOL_CARRY, POOL_WIDTH), F32),
            pltpu.VMEM((tile, ATTN_WIDTH), BF16),
            pltpu.VMEM(w_in.shape[1:], BF16),
            pltpu.VMEM(w_out.shape[1:], BF16),
        ],
        casts=casts,
    )


def _softplus(x):
    return jnp.maximum(x, 0.0) + jnp.log(1.0 + jnp.exp(-jnp.abs(x)))


def _odd_kernel(x_ref, nw_ref, win_ref, cw_ref, cb_ref, dtb_ref, alog_ref, dexp_ref, gnw_ref, wout_ref, o_ref,
                state_ref, ccarry_ref, act_ref, z_ref, y_ref, *, tile, inner, heads):
    t = pl.program_id(1)
    L = SSD_CHUNK
    nchunk = tile // L
    gn = SSD_GROUPS * SSD_STATE
    conv_dim = inner + 2 * gn
    heads_per_group = heads // SSD_GROUPS
    pairs_per_group = heads_per_group // 2
    gw = heads_per_group * SSD_HEAD_DIM

    @pl.when(t == 0)
    def _():
        state_ref[...] = jnp.zeros_like(state_ref)
        ccarry_ref[...] = jnp.zeros_like(ccarry_ref)

    x = x_ref[0]
    hn = _rmsnorm(x, nw_ref[...]).astype(BF16)

    def conv_cols(c0, width):
        xb = _dot(hn, win_ref[:, inner + c0:inner + c0 + width])
        xe = jnp.concatenate([ccarry_ref[:, c0:c0 + width], xb], axis=0)
        ccarry_ref[:, c0:c0 + width] = xb[tile - CONV_CARRY:]
        x1 = pltpu.roll(xe, 1, 0)
        w0, w1, w2, w3 = (cw_ref[kk:kk + 1, c0:c0 + width] for kk in range(SSD_CONV))
        acc = (xe * w3 + x1 * w2) + pltpu.roll(xe * w1 + x1 * w0, 2, 0)
        act_ref[:, c0:c0 + width] = _silu(acc[CONV_CARRY:] + cb_ref[:, c0:c0 + width])

    def z_cols(c0, width):
        z_ref[:, c0:c0 + width] = _dot(hn, win_ref[:, c0:c0 + width])

    def project_group_items(g):
        items = []
        for c0 in range(g * gw, (g + 1) * gw, PIECE_COLS):
            items.append(functools.partial(conv_cols, c0, PIECE_COLS))
            items.append(functools.partial(z_cols, c0, PIECE_COLS))
        return items

    def finish_group_items(g):
        g0 = g * gw
        cell = []

        def gate():
            yg = y_ref[:, g0:g0 + gw] + act_ref[:, g0:g0 + gw] * dexp_ref[:, g0:g0 + gw]
            yg = yg * _silu(z_ref[:, g0:g0 + gw])
            ms = jnp.mean(yg * yg, axis=-1, keepdims=True)
            cell.append((yg * lax.rsqrt(ms + EPS) * gnw_ref[:, g0:g0 + gw]).astype(BF16))

        def out_proj():
            part = _dot(cell.pop(), wout_ref[g0:g0 + gw, :])
            if g == 0:
                o_ref[0] = x + part
            else:
                o_ref[0] = o_ref[0] + part

        return [gate, out_proj]

    fill = [functools.partial(conv_cols, c0, PIECE_COLS) for c0 in range(inner, conv_dim, PIECE_COLS)]
    fill += project_group_items(0)

    def emit_fill(n):
        for _ in range(n):
            if fill:
                fill.pop(0)()

    dtr = _dot(hn, win_ref[:, inner + conv_dim:])
    emit_fill(1)

    a_neg = -jnp.exp(alog_ref[...])

    li = lax.broadcasted_iota(jnp.int32, (L, L), 0)
    si = lax.broadcasted_iota(jnp.int32, (L, L), 1)
    causal = si <= li
    tril = jnp.where(causal, 1.0, 0.0).astype(BF16)
    plane = lax.broadcasted_iota(jnp.int32, (L, LANES), 1)
    p_lo = plane < SSD_HEAD_DIM

    chunks = []
    for c in range(nchunk):
        r0 = c * L
        dt = _softplus(dtr[r0:r0 + L] + dtb_ref[...])
        a_cs = sum(_dot(tril, part) for part in _split3(dt * a_neg))
        emit_fill(1)
        a2 = a_cs * LOG2E
        w = dt * jnp.exp(a_cs[L - 1:L, :] - a_cs)
        shift = a2 - jnp.log2(dt)
        chunks.append(dict(a2=a2, wT=w.T[:heads], a2_shiftT=shift.T[:heads]))
        emit_fill(1)
    emit_fill(len(fill))

    pending = []
    pairs_per_step = nchunk * pairs_per_group
    for g in range(SSD_GROUPS):
        if g + 1 < SSD_GROUPS:
            pending.extend(project_group_items(g + 1))
        n_items, n_pairs = len(pending), 0
        g0 = g * gw
        for c, ck in enumerate(chunks):
            r0 = c * L
            bg = act_ref[r0:r0 + L, inner + g * SSD_STATE:inner + (g + 1) * SSD_STATE]
            cg = act_ref[r0:r0 + L, inner + gn + g * SSD_STATE:inner + gn + (g + 1) * SSD_STATE].astype(BF16)
            bgT = bg.T.astype(BF16)
            cb = _dot(cg, bgT).astype(BF16)
            y_off = _dot(cg, state_ref[:, g0:g0 + gw].astype(BF16))
            for pr in range(pairs_per_group):
                h0 = g * heads_per_group + 2 * pr
                c0 = h0 * SSD_HEAD_DIM
                xp = act_ref[r0:r0 + L, c0:c0 + LANES]
                w_a = jnp.concatenate([jnp.where(p_lo, xp, 0.0), jnp.where(p_lo, 0.0, xp)], axis=0).astype(BF16)
                m_parts, b_parts, e_parts = [], [], []
                for hd in (h0, h0 + 1):
                    acol = jnp.broadcast_to(ck["a2"][:, hd:hd + 1], (L, L))
                    dec = jnp.exp2(jnp.where(causal, acol - ck["a2_shiftT"][hd:hd + 1, :], NEG))
                    m_parts.append(cb * dec.astype(BF16))
                    b_parts.append(bgT * ck["wT"][hd:hd + 1, :].astype(BF16))
                    e_parts.append(jnp.exp2(acol))
                lhs = jnp.concatenate([jnp.concatenate(m_parts, axis=1),
                                       jnp.concatenate(b_parts, axis=1)], axis=0)
                ra = _dot(lhs, w_a)
                ea_pair = jnp.where(p_lo, e_parts[0], e_parts[1])
                y_ref[r0:r0 + L, c0:c0 + LANES] = ra[:L] + y_off[:, c0 - g0:c0 - g0 + LANES] * ea_pair
                state_ref[:, c0:c0 + LANES] = state_ref[:, c0:c0 + LANES] * ea_pair[L - 1:L, :] + ra[L:]
                n_pairs += 1
                while len(pending) > n_items - n_pairs * n_items // pairs_per_step:
                    pending.pop(0)()
        pending.extend(finish_group_items(g))
    while pending:
        pending.pop(0)()


def _odd_mixer(x, nw, w_in, cw, cb, dtb, alog, dexp, gnw, w_out, *, heads, tile, casts=()):
    b, s, d = x.shape
    inner = heads * SSD_HEAD_DIM
    conv_dim = inner + 2 * SSD_GROUPS * SSD_STATE
    assert s % tile == 0 and tile % SSD_CHUNK == 0 and heads % (2 * SSD_GROUPS) == 0 and heads <= LANES
    assert inner % PIECE_COLS == 0 and w_in.shape == (d, inner + conv_dim + LANES)
    operands = (x, nw, w_in, cw, cb, dtb, alog, dexp, gnw, w_out)
    return _fused_call(
        functools.partial(_odd_kernel, tile=tile, inner=inner, heads=heads),
        name="odd_mixer",
        grid=(b, s // tile),
        in_specs=[pl.BlockSpec((1, tile, d), lambda i, j: (i, j, 0))]
        + [_const_spec(c.shape) for c in operands[1:]],
        operands=operands,
        out_spec=pl.BlockSpec((1, tile, d), lambda i, j: (i, j, 0)),
        out_shape=jax.ShapeDtypeStruct((b, s, d), F32),
        scratch_shapes=[
            pltpu.VMEM((SSD_STATE, inner), F32),
            pltpu.VMEM((CONV_CARRY, conv_dim), F32),
            pltpu.VMEM((tile, conv_dim), F32),
            pltpu.VMEM((tile, inner), F32),
            pltpu.VMEM((tile, inner), F32),
        ],
        casts=casts,
    )


def _row(v):
    return v.reshape(1, -1).astype(F32)


def kernel(x, mix_norm_w, ap_w_in, ap_sinks, pool_w, pool_scale, ap_w_out, ssd_w_in, ssd_conv_w, ssd_conv_b, ssd_dt_bias, ssd_A_log, ssd_D, ssd_norm_w, ssd_w_out, ffn_norm_w, w_gate, w_up, w_down, final_norm_w):
    b, s, d = x.shape
    depth = mix_norm_w.shape[0]
    heads = ssd_dt_bias.shape[1]
    even_tile = min(EVEN_TILE, s)
    odd_tile = min(ODD_TILE, s)
    ffn_tile = min(FFN_TILE, b * s)

    ssd_w_in_t = jnp.swapaxes(ssd_w_in, 1, 2)

    def ffn_casts(layer):
        return (_Cast(w_gate, layer), _Cast(w_up, layer), _Cast(w_down, layer))

    def ssd_casts(i):
        return (_Cast(ssd_w_in_t, i, transpose=True), _Cast(ssd_w_out, i))

    def lane_pad(a):
        return jnp.pad(a, ((0, 0), (0, LANES - heads)))

    for layer in range(depth):
        i = layer // 2
        nw = _row(mix_norm_w[layer])
        if layer % 2 == 0:
            sink_rep = jnp.broadcast_to(
                (ap_sinks[i].astype(F32) * LOG2E).reshape(ATTN_KV_HEADS, ATTN_GROUP, 1, 1),
                (ATTN_KV_HEADS, ATTN_GROUP, ATTN_BLOCK, LANES)).reshape(ATTN_KV_HEADS, ATTN_GROUP * ATTN_BLOCK, LANES)
            carried = ffn_casts(layer) + (ssd_casts(i) if layer + 1 < depth else ())
            x, ready = _even_mixer(x, nw, ap_w_in, sink_rep, pool_w[i].astype(BF16), _row(pool_scale[i]),
                                   ap_w_out, i, tile=even_tile, casts=carried)
            ffn_w, ssd_w = ready[:3], ready[3:]
        else:
            x, ffn_w = _odd_mixer(x, nw, ssd_w[0], ssd_conv_w[i].astype(F32), _row(ssd_conv_b[i]),
                                  lane_pad(_row(ssd_dt_bias[i])), lane_pad(_row(ssd_A_log[i])),
                                  _row(jnp.repeat(ssd_D[i], SSD_HEAD_DIM)), _row(ssd_norm_w[i]),
                                  ssd_w[1], heads=heads, tile=odd_tile, casts=ffn_casts(layer))
        x = _ffn(x.reshape(b * s, d), _row(ffn_norm_w[layer]), *ffn_w, _row(final_norm_w),
                 final=layer == depth - 1, tile=ffn_tile).reshape(b, s, d)
    return x
```
